```python
import math
import jax, jax.numpy as jnp
from jax import lax
import numpy as np

D_MODEL = 2048
BATCH = 4
SEQ = 4096
DEPTH = 4
DEC_BATCH = 4
DEC_SEQ = 2048
PAST_LEN = 128

N_MIXERS = 3
FNET_GROUPS = 8
FNET_GROUP_DIM = D_MODEL // FNET_GROUPS
CONV_WIDTH = 31
CONV_PAD = CONV_WIDTH // 2
MLA_HEADS = 16
MLA_Q_RANK = 512
MLA_KV_RANK = 512
MLA_NOPE_DIM = 128
MLA_ROPE_DIM = 64
MLA_V_DIM = 128
MLA_QK_DIM = MLA_NOPE_DIM + MLA_ROPE_DIM
MLA_DOWN_DIM = MLA_Q_RANK + MLA_KV_RANK + MLA_ROPE_DIM
MLA_SCALE = 1.0 / math.sqrt(MLA_QK_DIM)
ROPE_BASE = 10000.0
Q_BLOCK = 128
D_FF = ((8 * D_MODEL + 3 * 256 - 1) // (3 * 256)) * 256
N_FOURIER = (DEPTH + 2) // 3
N_CONV = (DEPTH + 1) // 3
N_MLA = DEPTH // 3
EPS = 1e-6

kernel_name = 'hybrid_fnet_conformer_mla_encoder'


def rms_norm(x, g):
    xf = x.astype(jnp.float32)
    y = xf * lax.rsqrt(jnp.mean(xf * xf, axis=-1, keepdims=True) + EPS)
    return (y * g.astype(jnp.float32)).astype(x.dtype)


def layer_norm(x, g, b):
    xf = x.astype(jnp.float32)
    mu = jnp.mean(xf, axis=-1, keepdims=True)
    var = jnp.mean(jnp.square(xf - mu), axis=-1, keepdims=True)
    y = (xf - mu) * lax.rsqrt(var + EPS)
    return (y * g.astype(jnp.float32) + b.astype(jnp.float32)).astype(x.dtype)


def modulate(h, shift, scale):
    return h * (1 + scale[:, None, :]) + shift[:, None, :]


def rope_tables(seq_len):
    inv_freq = 1.0 / (ROPE_BASE ** (jnp.arange(0, MLA_ROPE_DIM, 2, dtype=jnp.float32) / MLA_ROPE_DIM))
    ang = jnp.arange(seq_len, dtype=jnp.float32)[:, None] * inv_freq[None, :]
    ang = jnp.concatenate([ang, ang], axis=-1)
    return jnp.cos(ang), jnp.sin(ang)


def apply_rope(x, cos, sin):
    xf = x.astype(jnp.float32)
    half = MLA_ROPE_DIM // 2
    rot = jnp.concatenate([-xf[..., half:], xf[..., :half]], axis=-1)
    return (xf * cos + rot * sin).astype(x.dtype)


def fourier_mixer(h, w_o, b_o):
    B, S, D = h.shape
    hg = h.astype(jnp.float32).reshape(B, S, FNET_GROUPS, FNET_GROUP_DIM)
    f = jnp.fft.fft2(hg, axes=(1, 3), norm='ortho').real
    return f.reshape(B, S, D).astype(h.dtype) @ w_o + b_o


def conv_mixer(h, w_in, b_in, w_dw, b_dw, ln_g, ln_b, w_out, b_out):
    D = h.shape[-1]
    u = h @ w_in + b_in
    a, g = jnp.split(u, 2, axis=-1)
    u = a * jax.nn.sigmoid(g)
    u = lax.conv_general_dilated(
        u, w_dw[:, None, :], window_strides=(1,), padding=[(CONV_PAD, CONV_PAD)],
        dimension_numbers=('NWC', 'WIO', 'NWC'), feature_group_count=D) + b_dw
    u = jax.nn.silu(layer_norm(u, ln_g, ln_b))
    return u @ w_out + b_out


def mla_mixer(h, w_down, q_norm, w_uq, kv_norm, w_ukv, w_o):
    B, S, _ = h.shape
    lat = h @ w_down
    c_q = lat[..., :MLA_Q_RANK]
    c_kv = lat[..., MLA_Q_RANK:MLA_Q_RANK + MLA_KV_RANK]
    k_pe = lat[..., MLA_Q_RANK + MLA_KV_RANK:]
    q = (rms_norm(c_q, q_norm) @ w_uq).reshape(B, S, MLA_HEADS, MLA_QK_DIM)
    kv = (rms_norm(c_kv, kv_norm) @ w_ukv).reshape(B, S, MLA_HEADS, MLA_NOPE_DIM + MLA_V_DIM)
    q_nope, q_pe = q[..., :MLA_NOPE_DIM], q[..., MLA_NOPE_DIM:]
    k_nope, v = kv[..., :MLA_NOPE_DIM], kv[..., MLA_NOPE_DIM:]
    cos, sin = rope_tables(S)
    q_pe = apply_rope(q_pe, cos[:, None, :], sin[:, None, :])
    k_pe = apply_rope(k_pe, cos, sin)
    nblk = S // Q_BLOCK
    qn_b = q_nope.reshape(B, nblk, Q_BLOCK, MLA_HEADS, MLA_NOPE_DIM).transpose(1, 0, 2, 3, 4)
    qp_b = q_pe.reshape(B, nblk, Q_BLOCK, MLA_HEADS, MLA_ROPE_DIM).transpose(1, 0, 2, 3, 4)

    def attend(blk):
        qn, qp = blk
        s = (jnp.einsum('bqhd,bkhd->bhqk', qn, k_nope) +
             jnp.einsum('bqhr,bkr->bhqk', qp, k_pe))
        p = jax.nn.softmax(s.astype(jnp.float32) * MLA_SCALE, axis=-1).astype(v.dtype)
        return jnp.einsum('bhqk,bkhd->bqhd', p, v)

    o = lax.map(attend, (qn_b, qp_b))
    o = o.transpose(1, 0, 2, 3, 4).reshape(B, S, MLA_HEADS * MLA_V_DIM)
    return o @ w_o


def swiglu(h, w_gate, w_up, w_down):
    return (jax.nn.silu(h @ w_gate) * (h @ w_up)) @ w_down


def trunk(x, c, ln_mix, ln_ffn, w_ada, b_ada, fnet_w_o, fnet_b_o,
          conv_w_in, conv_b_in, conv_w_dw, conv_b_dw, conv_ln_g, conv_ln_b, conv_w_out, conv_b_out,
          mla_w_down, mla_q_norm, mla_w_uq, mla_kv_norm, mla_w_ukv, mla_w_o,
          ffn_w_gate, ffn_w_up, ffn_w_down, final_norm):
    c_act = jax.nn.silu(c)
    for i in range(DEPTH):
        mod = c_act @ w_ada[i] + b_ada[i]
        sh_m, sc_m, g_m, sh_f, sc_f, g_f = jnp.split(mod, 6, axis=-1)
        h = modulate(rms_norm(x, ln_mix[i]), sh_m, sc_m)
        kind, j = i % N_MIXERS, i // N_MIXERS
        if kind == 0:
            y = fourier_mixer(h, fnet_w_o[j], fnet_b_o[j])
        elif kind == 1:
            y = conv_mixer(h, conv_w_in[j], conv_b_in[j], conv_w_dw[j], conv_b_dw[j],
                           conv_ln_g[j], conv_ln_b[j], conv_w_out[j], conv_b_out[j])
        else:
            y = mla_mixer(h, mla_w_down[j], mla_q_norm[j], mla_w_uq[j], mla_kv_norm[j],
                          mla_w_ukv[j], mla_w_o[j])
        x = x + g_m[:, None, :] * y
        h = modulate(rms_norm(x, ln_ffn[i]), sh_f, sc_f)
        x = x + g_f[:, None, :] * swiglu(h, ffn_w_gate[i], ffn_w_up[i], ffn_w_down[i])
    return rms_norm(x, final_norm)


def setup_inputs(seed: int = 0) -> dict:
    key = jax.random.key(seed)
    ks = jax.random.split(key, 32)

    def nrm(k, shape, scale):
        return jax.random.normal(k, shape, jnp.float32) * scale

    def gain(k, shape):
        return 1.0 + 0.05 * jax.random.normal(k, shape, jnp.float32)

    D = D_MODEL
    return {
        'x_prompt': nrm(ks[0], (BATCH, SEQ, D), 1.0),
        'x_sample': nrm(ks[1], (DEC_BATCH, DEC_SEQ, D), 1.0),
        'c_prompt': nrm(ks[2], (BATCH, D), 1.0),
        'c_sample': nrm(ks[3], (DEC_BATCH, D), 1.0),
        'ln_mix': gain(ks[4], (DEPTH, D)),
        'ln_ffn': gain(ks[5], (DEPTH, D)),
        'w_ada': nrm(ks[6], (DEPTH, D, 6 * D), D ** -0.5),
        'b_ada': nrm(ks[7], (DEPTH, 6 * D), 0.02),
        'fnet_w_o': nrm(ks[8], (N_FOURIER, D, D), D ** -0.5),
        'fnet_b_o': nrm(ks[9], (N_FOURIER, D), 0.02),
        'conv_w_in': nrm(ks[10], (N_CONV, D, 2 * D), D ** -0.5),
        'conv_b_in': nrm(ks[11], (N_CONV, 2 * D), 0.02),
        'conv_w_dw': nrm(ks[12], (N_CONV, CONV_WIDTH, D), CONV_WIDTH ** -0.5),
        'conv_b_dw': nrm(ks[13], (N_CONV, D), 0.02),
        'conv_ln_g': gain(ks[14], (N_CONV, D)),
        'conv_ln_b': nrm(ks[15], (N_CONV, D), 0.02),
        'conv_w_out': nrm(ks[16], (N_CONV, D, D), D ** -0.5),
        'conv_b_out': nrm(ks[17], (N_CONV, D), 0.02),
        'mla_w_down': nrm(ks[18], (N_MLA, D, MLA_DOWN_DIM), D ** -0.5),
        'mla_q_norm': gain(ks[19], (N_MLA, MLA_Q_RANK)),
        'mla_w_uq': nrm(ks[20], (N_MLA, MLA_Q_RANK, MLA_HEADS * MLA_QK_DIM), MLA_Q_RANK ** -0.5),
        'mla_kv_norm': gain(ks[21], (N_MLA, MLA_KV_RANK)),
        'mla_w_ukv': nrm(ks[22], (N_MLA, MLA_KV_RANK, MLA_HEADS * (MLA_NOPE_DIM + MLA_V_DIM)), MLA_KV_RANK ** -0.5),
        'mla_w_o': nrm(ks[23], (N_MLA, MLA_HEADS * MLA_V_DIM, D), (MLA_HEADS * MLA_V_DIM) ** -0.5),
        'ffn_w_gate': nrm(ks[24], (DEPTH, D, D_FF), D ** -0.5),
        'ffn_w_up': nrm(ks[25], (DEPTH, D, D_FF), D ** -0.5),
        'ffn_w_down': nrm(ks[26], (DEPTH, D_FF, D), D_FF ** -0.5),
        'final_norm': gain(ks[27], (D,)),
    }


def reference(x_prompt, x_sample, c_prompt, c_sample, ln_mix, ln_ffn, w_ada, b_ada,
              fnet_w_o, fnet_b_o, conv_w_in, conv_b_in, conv_w_dw, conv_b_dw, conv_ln_g, conv_ln_b,
              conv_w_out, conv_b_out, mla_w_down, mla_q_norm, mla_w_uq, mla_kv_norm, mla_w_ukv, mla_w_o,
              ffn_w_gate, ffn_w_up, ffn_w_down, final_norm):
    params = (ln_mix, ln_ffn, w_ada, b_ada, fnet_w_o, fnet_b_o,
              conv_w_in, conv_b_in, conv_w_dw, conv_b_dw, conv_ln_g, conv_ln_b, conv_w_out, conv_b_out,
              mla_w_down, mla_q_norm, mla_w_uq, mla_kv_norm, mla_w_ukv, mla_w_o,
              ffn_w_gate, ffn_w_up, ffn_w_down, final_norm)
    y_prompt = trunk(x_prompt, c_prompt, *params)
    y_sample = trunk(x_sample, c_sample, *params)
    return (y_prompt, y_sample)
```

```python
import functools
import math
from typing import NamedTuple

import jax
import jax.numpy as jnp
from jax import lax
from jax.experimental import pallas as pl
from jax.experimental.pallas import tpu as pltpu

F32 = jnp.float32
BF16 = jnp.bfloat16

EPS = 1e-6
N_MIXERS = 3
FNET_GROUPS = 8
CONV_WIDTH = 31
CONV_PAD = CONV_WIDTH // 2
CONV_HALO = 16
MLA_HEADS = 16
MLA_Q_RANK = 512
MLA_KV_RANK = 512
MLA_NOPE_DIM = 128
MLA_ROPE_DIM = 64
MLA_V_DIM = 128
MLA_QK_DIM = MLA_NOPE_DIM + MLA_ROPE_DIM
MLA_SCALE = 1.0 / math.sqrt(MLA_QK_DIM)
ROPE_BASE = 10000.0
LANES = 128
ROPE_PAD = LANES
QK_PAD = MLA_NOPE_DIM + ROPE_PAD
DFT_Q = 128
VMEM_LIMIT = 48 * 1024 * 1024


class Geo(NamedTuple):
    B1: int
    S1: int
    B2: int
    S2: int
    D: int

    @property
    def T1(self):
        return self.B1 * self.S1

    @property
    def T(self):
        return self.B1 * self.S1 + self.B2 * self.S2

    @property
    def NB(self):
        return self.B1 + self.B2


def _cparams(sem, vmem=VMEM_LIMIT):
    return pltpu.CompilerParams(dimension_semantics=sem, vmem_limit_bytes=vmem)


def _row_tile(geo, want):
    g = math.gcd(geo.S1, geo.S2)
    t = min(want, g)
    while g % t:
        t //= 2
    return t


def _seq_idx(i, tm, geo):
    n1 = geo.T1 // tm
    return jnp.where(i < n1, i // (geo.S1 // tm), geo.B1 + (i - n1) // (geo.S2 // tm))


def _sigmoid(x):
    return 1.0 / (1.0 + jnp.exp(-x))


def _norm_mod(x, lnw, sc, sh):
    ms = jnp.mean(x * x, axis=-1, keepdims=True)
    y = x * lax.rsqrt(ms + EPS) * lnw
    return y * (1.0 + sc) + sh


def _ada_kernel(c_ref, w_ref, b_ref, o_ref):
    c = c_ref[...]
    a = (c * _sigmoid(c)).astype(BF16)
    o_ref[...] = jnp.dot(a, w_ref[...].astype(BF16), preferred_element_type=F32) + b_ref[...]


def _ada(c_pad, w_ada, b_ada):
    L, D, N = w_ada.shape
    R = c_pad.shape[0]
    tn = min(1024, N)
    return pl.pallas_call(
        _ada_kernel,
        grid=(L, N // tn),
        in_specs=[
            pl.BlockSpec((R, D), lambda l, j: (0, 0)),
            pl.BlockSpec((None, D, tn), lambda l, j: (l, 0, j)),
            pl.BlockSpec((None, 1, tn), lambda l, j: (l, 0, j)),
        ],
        out_specs=pl.BlockSpec((None, R, tn), lambda l, j: (l, 0, j)),
        out_shape=jax.ShapeDtypeStruct((L, R, N), F32),
        compiler_params=_cparams(("parallel", "parallel")),
        name="ada_mod",
    )(c_pad, w_ada, b_ada.reshape(L, 1, N))


def _dual_kernel(x_ref, lnw_ref, sc_ref, sh_ref, w1_ref, w2_ref, *rest, mode, bias):
    if bias:
        b1_ref, b2_ref, o_ref, h_ref = rest
    else:
        o_ref, h_ref = rest

    @pl.when(pl.program_id(1) == 0)
    def _():
        h_ref[...] = _norm_mod(x_ref[...], lnw_ref[...], sc_ref[...], sh_ref[...]).astype(BF16)

    h = h_ref[...]
    p1 = jnp.dot(h, w1_ref[...], preferred_element_type=F32)
    p2 = jnp.dot(h, w2_ref[...], preferred_element_type=F32)
    if bias:
        p1 = p1 + b1_ref[...]
        p2 = p2 + b2_ref[...]
    if mode == "swiglu":
        o = p1 * _sigmoid(p1) * p2
    else:
        o = p1 * _sigmoid(p2)
    o_ref[...] = o.astype(o_ref.dtype)


def _dual(geo, x, lnw, sc, sh, w1, off1, w2, off2, n_out, b1, b2, *, mode, out_dtype, tm_want=1024, tn_want=512):
    T, D = x.shape
    tm = _row_tile(geo, tm_want)
    tn = min(tn_want, n_out)
    assert n_out % tn == 0 and off1 % tn == 0 and off2 % tn == 0
    o1, o2 = off1 // tn, off2 // tn
    bias = b1 is not None
    mod_spec = pl.BlockSpec((None, 1, D), lambda i, j: (_seq_idx(i, tm, geo), 0, 0))
    in_specs = [
        pl.BlockSpec((tm, D), lambda i, j: (i, 0)),
        pl.BlockSpec((1, D), lambda i, j: (0, 0)),
        mod_spec,
        mod_spec,
        pl.BlockSpec((D, tn), lambda i, j: (0, j + o1)),
        pl.BlockSpec((D, tn), lambda i, j: (0, j + o2)),
    ]
    args = [x, lnw, sc, sh, w1, w2]
    if bias:
        in_specs += [pl.BlockSpec((1, tn), lambda i, j: (0, j + o1)),
                     pl.BlockSpec((1, tn), lambda i, j: (0, j + o2))]
        args += [b1, b2]
    return pl.pallas_call(
        functools.partial(_dual_kernel, mode=mode, bias=bias),
        grid=(T // tm, n_out // tn),
        in_specs=in_specs,
        out_specs=pl.BlockSpec((tm, tn), lambda i, j: (i, j)),
        out_shape=jax.ShapeDtypeStruct((T, n_out), out_dtype),
        scratch_shapes=[pltpu.VMEM((tm, D), BF16)],
        compiler_params=_cparams(("parallel", "arbitrary")),
        name="dual_" + mode,
    )(*args)


def _res_kernel(a_ref, w_ref, *rest, bias):
    if bias:
        b_ref, x_ref, g_ref, o_ref = rest
    else:
        x_ref, g_ref, o_ref = rest
    y = jnp.dot(a_ref[...], w_ref[...], preferred_element_type=F32)
    if bias:
        y = y + b_ref[...]
    o_ref[...] = x_ref[...] + g_ref[...] * y


def _res(geo, a, w, b, x, g, *, tm_want=1024, tn_want=512):
    T, K = a.shape
    D = w.shape[1]
    tm = _row_tile(geo, tm_want)
    tn = min(tn_want, D)
    bias = b is not None
    in_specs = [pl.BlockSpec((tm, K), lambda i, j: (i, 0)),
                pl.BlockSpec((K, tn), lambda i, j: (0, j))]
    args = [a, w]
    if bias:
        in_specs.append(pl.BlockSpec((1, tn), lambda i, j: (0, j)))
        args.append(b)
    in_specs += [pl.BlockSpec((tm, tn), lambda i, j: (i, j)),
                 pl.BlockSpec((None, 1, tn), lambda i, j: (_seq_idx(i, tm, geo), 0, j))]
    args += [x, g]
    return pl.pallas_call(
        functools.partial(_res_kernel, bias=bias),
        grid=(T // tm, D // tn),
        in_specs=in_specs,
        out_specs=pl.BlockSpec((tm, tn), lambda i, j: (i, j)),
        out_shape=jax.ShapeDtypeStruct((T, D), F32),
        input_output_aliases={len(args) - 2: 0},
        compiler_params=_cparams(("parallel", "parallel")),
        name="res_mm",
    )(*args)


def _fnet_chan_kernel(x_ref, lnw_ref, sc_ref, sh_ref, cs_ref, hc_ref, hs_ref, *, gd):
    h = _norm_mod(x_ref[...], lnw_ref[...], sc_ref[...], sh_ref[...]).astype(BF16)
    cs = cs_ref[...]
    for g in range(h.shape[1] // gd):
        lo, hi = g * gd, (g + 1) * gd
        p = jnp.dot(h[:, lo:hi], cs, preferred_element_type=F32)
        hc_ref[:, lo:hi] = p[:, :gd].astype(BF16)
        hs_ref[:, lo:hi] = p[:, gd:].astype(BF16)


def _fnet_chan(geo, x, lnw, sc, sh, cs, *, tm_want=512):
    T, D = x.shape
    gd = cs.shape[0]
    tm = _row_tile(geo, tm_want)
    mod_spec = pl.BlockSpec((None, 1, D), lambda i: (_seq_idx(i, tm, geo), 0, 0))
    return pl.pallas_call(
        functools.partial(_fnet_chan_kernel, gd=gd),
        grid=(T // tm,),
        in_specs=[pl.BlockSpec((tm, D), lambda i: (i, 0)),
                  pl.BlockSpec((1, D), lambda i: (0, 0)),
                  mod_spec, mod_spec,
                  pl.BlockSpec((gd, 2 * gd), lambda i: (0, 0))],
        out_specs=[pl.BlockSpec((tm, D), lambda i: (i, 0))] * 2,
        out_shape=[jax.ShapeDtypeStruct((T, D), BF16)] * 2,
        compiler_params=_cparams(("parallel",)),
        name="fnet_chan",
    )(x, lnw, sc, sh, cs)


def _fnet_seq_kernel(t1c_ref, t1s_ref, t2c_ref, t2s_ref, hc_ref, hs_ref, o_ref, wc_ref, ws_ref, *, tm):
    i = pl.program_id(0)

    @pl.when((pl.program_id(1) == 0) & (pl.program_id(2) == 0))
    def _():
        c2 = t2c_ref[...]
        s2 = t2s_ref[...]
        for u in range(tm // DFT_Q):
            p = i * (tm // DFT_Q) + u
            c1 = t1c_ref[p]
            s1 = t1s_ref[p]
            wc_ref[u * DFT_Q:(u + 1) * DFT_Q, :] = (c1 * c2 - s1 * s2).astype(BF16)
            ws_ref[u * DFT_Q:(u + 1) * DFT_Q, :] = (-(s1 * c2 + c1 * s2)).astype(BF16)

    acc = jnp.dot(wc_ref[...], hc_ref[...], preferred_element_type=F32)
    acc = acc + jnp.dot(ws_ref[...], hs_ref[...], preferred_element_type=F32)
    o_ref[...] = acc.astype(o_ref.dtype)


def _dft_tables(S):
    q = min(DFT_Q, S)
    s = jnp.arange(S, dtype=jnp.int32)[None, :]
    p = jnp.arange(S // q, dtype=jnp.int32)[:, None]
    r = jnp.arange(q, dtype=jnp.int32)[:, None]
    w = 2.0 * math.pi / S
    a1 = ((p * q * s) % S).astype(F32) * w
    a2 = ((r * s) % S).astype(F32) * w
    nrm = 1.0 / math.sqrt(S)
    t1c = (jnp.cos(a1) * nrm).reshape(S // q, 1, S)
    t1s = (jnp.sin(a1) * nrm).reshape(S // q, 1, S)
    return t1c, t1s, jnp.cos(a2), jnp.sin(a2)


def _fnet_seq(hc, hs, B, S, row_off, *, tm_want=512, tn_want=512):
    D = hc.shape[1]
    q = min(DFT_Q, S)
    assert q == DFT_Q or S < DFT_Q
    tm = min(tm_want, S)
    tn = min(tn_want, D)
    t1c, t1s, t2c, t2s = _dft_tables(S)
    P = S // q
    ob = row_off // S
    assert row_off % S == 0
    return pl.pallas_call(
        functools.partial(_fnet_seq_kernel, tm=tm),
        grid=(S // tm, B, D // tn),
        in_specs=[pl.BlockSpec((P, 1, S), lambda i, b, j: (0, 0, 0)),
                  pl.BlockSpec((P, 1, S), lambda i, b, j: (0, 0, 0)),
                  pl.BlockSpec((q, S), lambda i, b, j: (0, 0)),
                  pl.BlockSpec((q, S), lambda i, b, j: (0, 0)),
                  pl.BlockSpec((S, tn), lambda i, b, j: (ob + b, j)),
                  pl.BlockSpec((S, tn), lambda i, b, j: (ob + b, j))],
        out_specs=pl.BlockSpec((tm, tn), lambda i, b, j: (b * (S // tm) + i, j)),
        out_shape=jax.ShapeDtypeStruct((B * S, D), BF16),
        scratch_shapes=[pltpu.VMEM((tm, S), BF16), pltpu.VMEM((tm, S), BF16)],
        compiler_params=_cparams(("parallel", "arbitrary", "arbitrary")),
        name="fnet_seq",
    )(t1c, t1s, t2c, t2s, hc, hs)


def _conv_kernel(prev_ref, cur_ref, next_ref, w_ref, bdw_ref, lng_ref, lnb_ref, o_ref, buf_ref, cv_ref,
                 *, ts, geo, rc, cw):
    r0 = pl.program_id(0) * ts
    in1 = r0 < geo.T1
    S = jnp.where(in1, geo.S1, geo.S2)
    rel = jnp.where(in1, r0, r0 - geo.T1)
    first = (rel % S) == 0
    last = ((rel + ts) % S) == 0
    H = CONV_HALO
    buf_ref[0:H, :] = jnp.where(first, 0.0, prev_ref[...])
    buf_ref[H:H + ts, :] = cur_ref[...]
    buf_ref[H + ts:2 * H + ts, :] = jnp.where(last, 0.0, next_ref[...])

    D = cur_ref.shape[1]
    base = H - CONV_PAD
    for c in range(D // cw):
        cs = slice(c * cw, (c + 1) * cw)
        wts = [w_ref[k:k + 1, cs] for k in range(CONV_WIDTH)]
        bias = bdw_ref[:, cs]

        for r in range(ts // rc):
            row = r * rc
            acc = jnp.zeros((rc, cw), F32)
            for k in range(CONV_WIDTH):
                acc = acc + buf_ref[row + base + k:row + base + k + rc, cs] * wts[k]
            cv_ref[row:row + rc, cs] = acc + bias

    cvv = cv_ref[...]
    mu = jnp.mean(cvv, axis=-1, keepdims=True)
    d = cvv - mu
    var = jnp.mean(d * d, axis=-1, keepdims=True)
    y = d * lax.rsqrt(var + EPS) * lng_ref[...] + lnb_ref[...]
    o_ref[...] = (y * _sigmoid(y)).astype(o_ref.dtype)


def _conv(geo, u, w_dw, b_dw, ln_g, ln_b, *, ts_want=256):
    T, D = u.shape
    ts = _row_tile(geo, ts_want)
    H = CONV_HALO
    assert ts % H == 0
    nh = T // H
    rc = min(32, ts)
    cw = min(512, D)
    return pl.pallas_call(
        functools.partial(_conv_kernel, ts=ts, geo=geo, rc=rc, cw=cw),
        grid=(T // ts,),
        in_specs=[pl.BlockSpec((H, D), lambda i: (jnp.maximum(i * (ts // H) - 1, 0), 0)),
                  pl.BlockSpec((ts, D), lambda i: (i, 0)),
                  pl.BlockSpec((H, D), lambda i: (jnp.minimum((i + 1) * (ts // H), nh - 1), 0)),
                  pl.BlockSpec((CONV_WIDTH, D), lambda i: (0, 0)),
                  pl.BlockSpec((1, D), lambda i: (0, 0)),
                  pl.BlockSpec((1, D), lambda i: (0, 0)),
                  pl.BlockSpec((1, D), lambda i: (0, 0))],
        out_specs=pl.BlockSpec((ts, D), lambda i: (i, 0)),
        out_shape=jax.ShapeDtypeStruct((T, D), BF16),
        scratch_shapes=[pltpu.VMEM((ts + 2 * H, D), F32), pltpu.VMEM((ts, D), F32)],
        compiler_params=_cparams(("parallel",)),
        name="dwconv_ln_silu",
    )(u, u, u, w_dw, b_dw, ln_g, ln_b)


def _rope(pe, cos, sin):
    return pe * cos + pltpu.roll(pe, ROPE_PAD // 2, 1) * sin


def _mla_down_kernel(x_ref, lnw_ref, sc_ref, sh_ref, w_ref, qn_ref, kvn_ref, cos_ref, sin_ref,
                     cq_ref, ckv_ref, kpe_ref):
    h = _norm_mod(x_ref[...], lnw_ref[...], sc_ref[...], sh_ref[...]).astype(BF16)
    lat = jnp.dot(h, w_ref[...], preferred_element_type=F32)
    qr, kr = MLA_Q_RANK, MLA_KV_RANK
    cq = lat[:, :qr]
    cq_ref[...] = (cq * lax.rsqrt(jnp.mean(cq * cq, axis=-1, keepdims=True) + EPS) * qn_ref[...]).astype(BF16)
    ckv = lat[:, qr:qr + kr]
    ckv_ref[...] = (ckv * lax.rsqrt(jnp.mean(ckv * ckv, axis=-1, keepdims=True) + EPS) * kvn_ref[...]).astype(BF16)
    kpe_ref[...] = _rope(lat[:, qr + kr:], cos_ref[...], sin_ref[...]).astype(BF16)


def _mla_down(geo, x, lnw, sc, sh, w_down_p, q_norm, kv_norm, cos_t, sin_t, *, tm_want=512):
    T, D = x.shape
    N = w_down_p.shape[1]
    tm = _row_tile(geo, tm_want)
    mod_spec = pl.BlockSpec((None, 1, D), lambda i: (_seq_idx(i, tm, geo), 0, 0))
    row = lambda n: pl.BlockSpec((tm, n), lambda i: (i, 0))
    full = lambda r, n: pl.BlockSpec((r, n), lambda i: (0, 0))
    return pl.pallas_call(
        _mla_down_kernel,
        grid=(T // tm,),
        in_specs=[row(D), full(1, D), mod_spec, mod_spec, full(D, N),
                  full(1, MLA_Q_RANK), full(1, MLA_KV_RANK), row(ROPE_PAD), row(ROPE_PAD)],
        out_specs=[row(MLA_Q_RANK), row(MLA_KV_RANK), row(ROPE_PAD)],
        out_shape=[jax.ShapeDtypeStruct((T, MLA_Q_RANK), BF16),
                   jax.ShapeDtypeStruct((T, MLA_KV_RANK), BF16),
                   jax.ShapeDtypeStruct((T, ROPE_PAD), BF16)],
        compiler_params=_cparams(("parallel",)),
        name="mla_down",
    )(x, lnw, sc, sh, w_down_p, q_norm, kv_norm, cos_t, sin_t)


def _mla_q_kernel(cq_ref, w_ref, cos_ref, sin_ref, q_ref):
    a = cq_ref[...]
    cos = cos_ref[...] * MLA_SCALE
    sin = sin_ref[...] * MLA_SCALE
    for h in range(MLA_HEADS):
        lo = h * QK_PAD
        acc = jnp.dot(a, w_ref[:, lo:lo + QK_PAD], preferred_element_type=F32)
        q_ref[:, lo:lo + MLA_NOPE_DIM] = (acc[:, :MLA_NOPE_DIM] * MLA_SCALE).astype(BF16)
        q_ref[:, lo + MLA_NOPE_DIM:lo + QK_PAD] = _rope(acc[:, MLA_NOPE_DIM:], cos, sin).astype(BF16)


def _mla_q(geo, cq, w_uq_p, cos_t, sin_t, *, tm_want=512):
    T, R = cq.shape
    N = w_uq_p.shape[1]
    tm = _row_tile(geo, tm_want)
    return pl.pallas_call(
        _mla_q_kernel,
        grid=(T // tm,),
        in_specs=[pl.BlockSpec((tm, R), lambda i: (i, 0)),
                  pl.BlockSpec((R, N), lambda i: (0, 0)),
                  pl.BlockSpec((tm, ROPE_PAD), lambda i: (i, 0)),
                  pl.BlockSpec((tm, ROPE_PAD), lambda i: (i, 0))],
        out_specs=pl.BlockSpec((tm, N), lambda i: (i, 0)),
        out_shape=jax.ShapeDtypeStruct((T, N), BF16),
        compiler_params=_cparams(("parallel",)),
        name="mla_q_up",
    )(cq, w_uq_p, cos_t, sin_t)


def _mla_kv_kernel(ckv_ref, w_ref, kpe_ref, k_ref, v_ref):
    a = ckv_ref[...]
    kpe = kpe_ref[...]
    hw = MLA_NOPE_DIM + MLA_V_DIM
    for h in range(MLA_HEADS):
        acc = jnp.dot(a, w_ref[:, h * hw:(h + 1) * hw], preferred_element_type=F32)
        lo = h * QK_PAD
        k_ref[:, lo:lo + MLA_NOPE_DIM] = acc[:, :MLA_NOPE_DIM].astype(BF16)
        k_ref[:, lo + MLA_NOPE_DIM:lo + QK_PAD] = kpe
        v_ref[:, h * MLA_V_DIM:(h + 1) * MLA_V_DIM] = acc[:, MLA_NOPE_DIM:].astype(BF16)


def _mla_kv(geo, ckv, w_ukv, kpe, *, tm_want=512):
    T, R = ckv.shape
    N = w_ukv.shape[1]
    tm = _row_tile(geo, tm_want)
    nk = MLA_HEADS * QK_PAD
    nv = MLA_HEADS * MLA_V_DIM
    return pl.pallas_call(
        _mla_kv_kernel,
        grid=(T // tm,),
        in_specs=[pl.BlockSpec((tm, R), lambda i: (i, 0)),
                  pl.BlockSpec((R, N), lambda i: (0, 0)),
                  pl.BlockSpec((tm, ROPE_PAD), lambda i: (i, 0))],
        out_specs=[pl.BlockSpec((tm, nk), lambda i: (i, 0)),
                   pl.BlockSpec((tm, nv), lambda i: (i, 0))],
        out_shape=[jax.ShapeDtypeStruct((T, nk), BF16),
                   jax.ShapeDtypeStruct((T, nv), BF16)],
        compiler_params=_cparams(("parallel",)),
        name="mla_kv_up",
    )(ckv, w_ukv, kpe)


def _attn_kernel(q_ref, k_ref, v_ref, o_ref):
    s = lax.dot_general(q_ref[...], k_ref[...], (((1,), (1,)), ((), ())), preferred_element_type=F32)
    m = jnp.max(s, axis=-1, keepdims=True)
    p = jnp.exp(s - m)
    l = jnp.sum(p, axis=-1, keepdims=True)
    o = jnp.dot(p.astype(BF16), v_ref[...], preferred_element_type=F32)
    o_ref[...] = (o / l).astype(o_ref.dtype)


def _attn(q, k, v, B, S, row_off, *, tq_want=512):
    tq = min(tq_want, S)
    assert row_off % S == 0
    ob = row_off // S
    oq = row_off // tq
    nq = S // tq
    return pl.pallas_call(
        _attn_kernel,
        grid=(B, MLA_HEADS, nq),
        in_specs=[pl.BlockSpec((tq, QK_PAD), lambda b, h, i: (oq + b * nq + i, h)),
                  pl.BlockSpec((S, QK_PAD), lambda b, h, i: (ob + b, h)),
                  pl.BlockSpec((S, MLA_V_DIM), lambda b, h, i: (ob + b, h))],
        out_specs=pl.BlockSpec((tq, MLA_V_DIM), lambda b, h, i: (b * nq + i, h)),
        out_shape=jax.ShapeDtypeStruct((B * S, MLA_HEADS * MLA_V_DIM), BF16),
        compiler_params=_cparams(("parallel", "parallel", "arbitrary")),
        name="mla_attn",
    )(q, k, v)


def _final_kernel(x_ref, w_ref, o_ref):
    x = x_ref[...]
    o_ref[...] = x * lax.rsqrt(jnp.mean(x * x, axis=-1, keepdims=True) + EPS) * w_ref[...]


def _final_norm(x, w, rows, row_off, *, tm_want=512):
    D = x.shape[1]
    tm = math.gcd(tm_want, math.gcd(rows, row_off)) if row_off else math.gcd(tm_want, rows)
    ob = row_off // tm
    return pl.pallas_call(
        _final_kernel,
        grid=(rows // tm,),
        in_specs=[pl.BlockSpec((tm, D), lambda i: (ob + i, 0)),
                  pl.BlockSpec((1, D), lambda i: (0, 0))],
        out_specs=pl.BlockSpec((tm, D), lambda i: (i, 0)),
        out_shape=jax.ShapeDtypeStruct((rows, D), F32),
        compiler_params=_cparams(("parallel",)),
        name="final_norm",
    )(x, w)


def _chan_dft_matrix(gd):
    k = jnp.arange(gd, dtype=jnp.int32)
    ang = ((k[:, None] * k[None, :]) % gd).astype(F32) * (2.0 * math.pi / gd)
    nrm = 1.0 / math.sqrt(gd)
    return jnp.concatenate([jnp.cos(ang) * nrm, jnp.sin(ang) * nrm], axis=1).astype(BF16)


def _rope_tables(geo):
    half = MLA_ROPE_DIM // 2
    inv_freq = 1.0 / (ROPE_BASE ** (jnp.arange(0, MLA_ROPE_DIM, 2, dtype=F32) / MLA_ROPE_DIM))
    pos = jnp.concatenate([jnp.tile(jnp.arange(geo.S1, dtype=F32), geo.B1),
                           jnp.tile(jnp.arange(geo.S2, dtype=F32), geo.B2)])
    ang = pos[:, None] * inv_freq[None, :]
    c, s = jnp.cos(ang), jnp.sin(ang)
    z = jnp.zeros_like(c)
    assert ROPE_PAD == 4 * half
    return jnp.concatenate([c, z, c, z], axis=1), jnp.concatenate([-s, z, s, z], axis=1)


def _spread_rope_cols(w):
    half = MLA_ROPE_DIM // 2
    z = jnp.zeros(w.shape[:-1] + (half,), w.dtype)
    return jnp.concatenate([w[..., :half], z, w[..., half:], z], axis=-1)


def kernel(x_prompt, x_sample, c_prompt, c_sample, ln_mix, ln_ffn, w_ada, b_ada, fnet_w_o, fnet_b_o, conv_w_in, conv_b_in, conv_w_dw, conv_b_dw, conv_ln_g, conv_ln_b, conv_w_out, conv_b_out, mla_w_down, mla_q_norm, mla_w_uq, mla_kv_norm, mla_w_ukv, mla_w_o, ffn_w_gate, ffn_w_up, ffn_w_down, final_norm):
    B1, S1, D = x_prompt.shape
    B2, S2, _ = x_sample.shape
    geo = Geo(B1, S1, B2, S2, D)
    depth = ln_mix.shape[0]
    T, T1, NB = geo.T, geo.T1, geo.NB

    x = jnp.concatenate([x_prompt.reshape(T1, D), x_sample.reshape(T - T1, D)], axis=0)

    c_all = jnp.concatenate([c_prompt, c_sample], axis=0)
    rpad = -NB % 16
    c_pad = jnp.pad(c_all, ((0, rpad), (0, 0)))
    mod = _ada(c_pad, w_ada, b_ada)[:, :NB].reshape(depth, NB, 6, 1, D)

    bf = lambda w: w.astype(BF16)
    row = lambda v: v.reshape(1, -1)
    cos_t = sin_t = None

    for i in range(depth):
        sh_m, sc_m, g_m, sh_f, sc_f, g_f = (mod[i, :, t] for t in range(6))
        kind, j = i % N_MIXERS, i // N_MIXERS
        lnw = row(ln_mix[i])
        if kind == 0:
            gd = D // FNET_GROUPS
            hc, hs = _fnet_chan(geo, x, lnw, sc_m, sh_m, _chan_dft_matrix(gd))
            f = jnp.concatenate([_fnet_seq(hc, hs, B1, S1, 0), _fnet_seq(hc, hs, B2, S2, T1)], axis=0)
            x = _res(geo, f, bf(fnet_w_o[j]), row(fnet_b_o[j]), x, g_m)
        elif kind == 1:
            w_in = bf(conv_w_in[j])
            b_in = row(conv_b_in[j])
            u = _dual(geo, x, lnw, sc_m, sh_m, w_in, 0, w_in, D, D, b_in, b_in, mode="glu", out_dtype=F32)
            u = _conv(geo, u, conv_w_dw[j], row(conv_b_dw[j]), row(conv_ln_g[j]), row(conv_ln_b[j]))
            x = _res(geo, u, bf(conv_w_out[j]), row(conv_b_out[j]), x, g_m)
        else:
            if cos_t is None:
                cos_t, sin_t = _rope_tables(geo)
            qk = MLA_Q_RANK + MLA_KV_RANK
            w_down_p = bf(jnp.concatenate([mla_w_down[j][:, :qk], _spread_rope_cols(mla_w_down[j][:, qk:])], axis=1))
            w_uq = mla_w_uq[j].reshape(MLA_Q_RANK, MLA_HEADS, MLA_QK_DIM)
            w_uq_p = bf(jnp.concatenate([w_uq[..., :MLA_NOPE_DIM], _spread_rope_cols(w_uq[..., MLA_NOPE_DIM:])],
                                        axis=-1).reshape(MLA_Q_RANK, MLA_HEADS * QK_PAD))
            cq, ckv, kpe = _mla_down(geo, x, lnw, sc_m, sh_m, w_down_p, row(mla_q_norm[j]), row(mla_kv_norm[j]),
                                     cos_t, sin_t)
            q = _mla_q(geo, cq, w_uq_p, cos_t, sin_t)
            k, v = _mla_kv(geo, ckv, bf(mla_w_ukv[j]), kpe)
            o = jnp.concatenate([_attn(q, k, v, B1, S1, 0), _attn(q, k, v, B2, S2, T1)], axis=0)
            x = _res(geo, o, bf(mla_w_o[j]), None, x, g_m)

        u = _dual(geo, x, row(ln_ffn[i]), sc_f, sh_f, bf(ffn_w_gate[i]), 0, bf(ffn_w_up[i]), 0,
                  ffn_w_gate.shape[2], None, None, mode="swiglu", out_dtype=BF16)
        x = _res(geo, u, bf(ffn_w_down[i]), None, x, g_f)

    fw = row(final_norm)
    y1 = _final_norm(x, fw, T1, 0).reshape(B1, S1, D)
    y2 = _final_norm(x, fw, T - T1, T1).reshape(B2, S2, D)
    return (y1, y2)
```

```python
import functools
import math
from typing import NamedTuple

import jax
import jax.numpy as jnp
from jax import lax
from jax.experimental import pallas as pl
from jax.experimental.pallas import tpu as pltpu

F32 = jnp.float32
BF16 = jnp.bfloat16

EPS = 1e-6
N_MIXERS = 3
FNET_GROUPS = 8
CONV_WIDTH = 31
CONV_PAD = CONV_WIDTH // 2
MLA_HEADS = 16
MLA_Q_RANK = 512
MLA_KV_RANK = 512
MLA_NOPE_DIM = 128
MLA_ROPE_DIM = 64
MLA_V_DIM = 128
MLA_QK_DIM = MLA_NOPE_DIM + MLA_ROPE_DIM
MLA_SCALE = 1.0 / math.sqrt(MLA_QK_DIM)
LOG2E = math.log2(math.e)
ROPE_BASE = 10000.0

LANES = 128
SUBLANES = 8
BF16_ROWS = 16
CONV_HALO = 2 * SUBLANES
CONV_STRIDE = 4
CONV_ROWS = CONV_STRIDE * SUBLANES
ROPE_PAD = LANES
QK_PAD = MLA_NOPE_DIM + ROPE_PAD
DFT_Q = 128
VMEM_LIMIT = 48 * 1024 * 1024

SH_M, SC_M, G_M, SH_F, SC_F, G_F = range(6)


class Geo(NamedTuple):
    B1: int
    S1: int
    B2: int
    S2: int
    D: int

    @property
    def T1(self):
        return self.B1 * self.S1

    @property
    def T2(self):
        return self.B2 * self.S2

    @property
    def T(self):
        return self.T1 + self.T2

    @property
    def NB(self):
        return self.B1 + self.B2


class Group(NamedTuple):
    B: int
    S: int
    row_off: int
    seq_off: int


def _groups(geo):
    return Group(geo.B1, geo.S1, 0, 0), Group(geo.B2, geo.S2, geo.T1, geo.B1)


def _cparams(sem, vmem=VMEM_LIMIT):
    return pltpu.CompilerParams(dimension_semantics=sem, vmem_limit_bytes=vmem)


def _row_tile(geo, want):
    g = math.gcd(geo.S1, geo.S2)
    t = min(want, g)
    while g % t:
        t //= 2
    return t


def _seq_idx(i, tm, geo):
    n1 = geo.T1 // tm
    return jnp.where(i < n1, i // (geo.S1 // tm), geo.B1 + (i - n1) // (geo.S2 // tm))


def _mod_spec(layer, which, seq_fn):
    def index_map(*g):
        return (layer, seq_fn(*g), which, 0, 0)
    return lambda D: pl.BlockSpec((None, None, None, 1, D), index_map)


def _layer_spec(layer, shape, tail_map, resident=False):
    mode = pl.Buffered(1) if resident else None
    return pl.BlockSpec((None,) + tuple(shape), lambda *g: (layer,) + tuple(tail_map(*g)), pipeline_mode=mode)


def _sigmoid(x):
    return 1.0 / (1.0 + jnp.exp(-x))


def _norm_mod(x, lnw, sc, sh):
    ms = jnp.mean(x * x, axis=-1, keepdims=True)
    y = x * lax.rsqrt(ms + EPS) * lnw
    return y * (1.0 + sc) + sh


def _ada_kernel(c_ref, w_ref, b_ref, o_ref):
    c = c_ref[...]
    a = (c * _sigmoid(c)).astype(BF16)
    o_ref[...] = jnp.dot(a, w_ref[...].astype(BF16), preferred_element_type=F32) + b_ref[...]


def _ada(c_pad, w_ada, b_ada):
    L, D, N = w_ada.shape
    R = c_pad.shape[0]
    tn = min(1024, N)
    return pl.pallas_call(
        _ada_kernel,
        grid=(L, N // tn),
        in_specs=[
            pl.BlockSpec((R, D), lambda l, j: (0, 0)),
            pl.BlockSpec((None, D, tn), lambda l, j: (l, 0, j)),
            pl.BlockSpec((None, 1, tn), lambda l, j: (l, 0, j)),
        ],
        out_specs=pl.BlockSpec((None, R, tn), lambda l, j: (l, 0, j)),
        out_shape=jax.ShapeDtypeStruct((L, R, N), F32),
        compiler_params=_cparams(("parallel", "parallel")),
        name="ada_mod",
    )(c_pad, w_ada, b_ada.reshape(L, 1, N))


def _dual_kernel(x_ref, lnw_ref, sc_ref, sh_ref, w1_ref, w2_ref, *rest, mode, bias):
    if bias:
        b1_ref, b2_ref, o_ref, h_ref = rest
    else:
        o_ref, h_ref = rest

    @pl.when(pl.program_id(1) == 0)
    def _():
        h_ref[...] = _norm_mod(x_ref[...], lnw_ref[...], sc_ref[...], sh_ref[...]).astype(BF16)

    h = h_ref[...]
    p1 = jnp.dot(h, w1_ref[...], preferred_element_type=F32)
    p2 = jnp.dot(h, w2_ref[...], preferred_element_type=F32)
    if bias:
        p1 = p1 + b1_ref[...]
        p2 = p2 + b2_ref[...]
    if mode == "swiglu":
        o = p1 * _sigmoid(p1) * p2
    else:
        o = p1 * _sigmoid(p2)
    o_ref[...] = o.astype(o_ref.dtype)


def _dual(geo, x, lnw, layer, mod, which_sc, which_sh, w1, lw1, off1, w2, lw2, off2, n_out, b, lb,
          *, mode, out_dtype, tm_want=1024, tn_want=512):
    T, D = x.shape
    tm = _row_tile(geo, tm_want)
    tn = min(tn_want, n_out)
    assert n_out % tn == 0 and off1 % tn == 0 and off2 % tn == 0
    o1, o2 = off1 // tn, off2 // tn
    bias = b is not None
    seq = lambda i, j: _seq_idx(i, tm, geo)
    in_specs = [
        pl.BlockSpec((tm, D), lambda i, j: (i, 0)),
        _layer_spec(layer, (1, D), lambda i, j: (0, 0)),
        _mod_spec(layer, which_sc, seq)(D),
        _mod_spec(layer, which_sh, seq)(D),
        _layer_spec(lw1, (D, tn), lambda i, j: (0, j + o1)),
        _layer_spec(lw2, (D, tn), lambda i, j: (0, j + o2)),
    ]
    args = [x, lnw, mod, mod, w1, w2]
    if bias:
        in_specs += [_layer_spec(lb, (1, tn), lambda i, j: (0, j + o1)),
                     _layer_spec(lb, (1, tn), lambda i, j: (0, j + o2))]
        args += [b, b]
    return pl.pallas_call(
        functools.partial(_dual_kernel, mode=mode, bias=bias),
        grid=(T // tm, n_out // tn),
        in_specs=in_specs,
        out_specs=pl.BlockSpec((tm, tn), lambda i, j: (i, j)),
        out_shape=jax.ShapeDtypeStruct((T, n_out), out_dtype),
        scratch_shapes=[pltpu.VMEM((tm, D), BF16)],
        compiler_params=_cparams(("parallel", "arbitrary")),
        name="dual_" + mode,
    )(*args)


def _res_kernel(*refs, bias, na, nx, n1):
    a_refs = refs[:na]
    w_ref = refs[na]
    pos = na + 1
    b_ref = None
    if bias:
        b_ref = refs[pos]
        pos += 1
    x_refs = refs[pos:pos + nx]
    g_ref, o_ref = refs[pos + nx], refs[pos + nx + 1]

    def compute(a_ref, x_ref):
        y = jnp.dot(a_ref[...], w_ref[...], preferred_element_type=F32)
        if bias:
            y = y + b_ref[...]
        o_ref[...] = x_ref[...] + g_ref[...] * y

    if na == 1 and nx == 1:
        compute(a_refs[0], x_refs[0])
    else:
        i = pl.program_id(0)
        pl.when(i < n1)(lambda: compute(a_refs[0], x_refs[0]))
        pl.when(i >= n1)(lambda: compute(a_refs[-1], x_refs[-1]))


def _res(geo, a_parts, w, lw, b, lb, x_parts, layer, mod, which_g, *, tm_want=1024, tn_want=512):
    K, D = w.shape[1], w.shape[2]
    T = geo.T
    tm = _row_tile(geo, tm_want)
    tn = min(tn_want, D)
    n1 = geo.T1 // tm
    bias = b is not None

    def part_specs(parts, width, col):
        if len(parts) == 1:
            return [pl.BlockSpec((tm, width), lambda i, j: (i, col(j)))]
        return [pl.BlockSpec((tm, width), lambda i, j: (jnp.minimum(i, n1 - 1), col(j))),
                pl.BlockSpec((tm, width), lambda i, j: (jnp.maximum(i - n1, 0), col(j)))]

    in_specs = part_specs(a_parts, K, lambda j: 0) + [_layer_spec(lw, (K, tn), lambda i, j: (0, j), resident=tn == D)]
    args = list(a_parts) + [w]
    if bias:
        in_specs.append(_layer_spec(lb, (1, tn), lambda i, j: (0, j)))
        args.append(b)
    in_specs += part_specs(x_parts, tn, lambda j: j)
    in_specs.append(pl.BlockSpec((None, None, None, 1, tn),
                                 lambda i, j: (layer, _seq_idx(i, tm, geo), which_g, 0, j)))
    alias = {len(args): 0} if len(x_parts) == 1 else {}
    args += list(x_parts) + [mod]
    return pl.pallas_call(
        functools.partial(_res_kernel, bias=bias, na=len(a_parts), nx=len(x_parts), n1=n1),
        grid=(T // tm, D // tn),
        in_specs=in_specs,
        out_specs=pl.BlockSpec((tm, tn), lambda i, j: (i, j)),
        out_shape=jax.ShapeDtypeStruct((T, D), F32),
        input_output_aliases=alias,
        compiler_params=_cparams(("parallel", "parallel")),
        name="res_mm",
    )(*args)


def _fnet_chan_kernel(x_ref, lnw_ref, sc_ref, sh_ref, cs_ref, hc_ref, hs_ref, *, gd):
    h = _norm_mod(x_ref[...], lnw_ref[...], sc_ref[...], sh_ref[...]).astype(BF16)
    cs = cs_ref[...]
    for g in range(h.shape[1] // gd):
        lo, hi = g * gd, (g + 1) * gd
        p = jnp.dot(h[:, lo:hi], cs, preferred_element_type=F32)
        hc_ref[:, lo:hi] = p[:, :gd].astype(BF16)
        hs_ref[:, lo:hi] = p[:, gd:].astype(BF16)


def _fnet_chan(x, row_off, grp, lnw, layer, mod, cs, *, tm_want=512):
    D = x.shape[1]
    gd = cs.shape[0]
    tm = min(tm_want, grp.S)
    assert row_off % tm == 0
    ob = row_off // tm
    rows = grp.B * grp.S
    seq = lambda i: grp.seq_off + i // (grp.S // tm)
    return pl.pallas_call(
        functools.partial(_fnet_chan_kernel, gd=gd),
        grid=(rows // tm,),
        in_specs=[pl.BlockSpec((tm, D), lambda i: (ob + i, 0)),
                  _layer_spec(layer, (1, D), lambda i: (0, 0)),
                  _mod_spec(layer, SC_M, seq)(D),
                  _mod_spec(layer, SH_M, seq)(D),
                  pl.BlockSpec((gd, 2 * gd), lambda i: (0, 0))],
        out_specs=[pl.BlockSpec((tm, D), lambda i: (i, 0))] * 2,
        out_shape=[jax.ShapeDtypeStruct((rows, D), BF16)] * 2,
        compiler_params=_cparams(("parallel",)),
        name="fnet_chan",
    )(x, lnw, mod, mod, cs)


def _fnet_seq_kernel(t1c_ref, t1s_ref, t2c_ref, t2s_ref, hc_ref, hs_ref, o_ref, wc_ref, ws_ref, *, tm):
    i = pl.program_id(0)

    @pl.when((pl.program_id(1) == 0) & (pl.program_id(2) == 0))
    def _():
        c2 = t2c_ref[...]
        s2 = t2s_ref[...]
        for u in range(tm // DFT_Q):
            p = i * (tm // DFT_Q) + u
            c1 = t1c_ref[p]
            s1 = t1s_ref[p]
            wc_ref[u * DFT_Q:(u + 1) * DFT_Q, :] = (c1 * c2 - s1 * s2).astype(BF16)
            ws_ref[u * DFT_Q:(u + 1) * DFT_Q, :] = (-(s1 * c2 + c1 * s2)).astype(BF16)

    acc = jnp.dot(wc_ref[...], hc_ref[...], preferred_element_type=F32)
    acc = acc + jnp.dot(ws_ref[...], hs_ref[...], preferred_element_type=F32)
    o_ref[...] = acc.astype(o_ref.dtype)


def _dft_tables(S):
    q = min(DFT_Q, S)
    s = jnp.arange(S, dtype=jnp.int32)[None, :]
    p = jnp.arange(S // q, dtype=jnp.int32)[:, None]
    r = jnp.arange(q, dtype=jnp.int32)[:, None]
    w = 2.0 * math.pi / S
    a1 = ((p * q * s) % S).astype(F32) * w
    a2 = ((r * s) % S).astype(F32) * w
    nrm = 1.0 / math.sqrt(S)
    t1c = (jnp.cos(a1) * nrm).reshape(S // q, 1, S)
    t1s = (jnp.sin(a1) * nrm).reshape(S // q, 1, S)
    return t1c, t1s, jnp.cos(a2), jnp.sin(a2)


def _fnet_seq(hc, hs, grp, *, tm_want=512, tn_want=512):
    B, S = grp.B, grp.S
    D = hc.shape[1]
    q = min(DFT_Q, S)
    tm = min(tm_want, S)
    tn = min(tn_want, D)
    assert tm % q == 0
    t1c, t1s, t2c, t2s = _dft_tables(S)
    P = S // q
    return pl.pallas_call(
        functools.partial(_fnet_seq_kernel, tm=tm),
        grid=(S // tm, B, D // tn),
        in_specs=[pl.BlockSpec((P, 1, S), lambda i, b, j: (0, 0, 0)),
                  pl.BlockSpec((P, 1, S), lambda i, b, j: (0, 0, 0)),
                  pl.BlockSpec((q, S), lambda i, b, j: (0, 0)),
                  pl.BlockSpec((q, S), lambda i, b, j: (0, 0)),
                  pl.BlockSpec((S, tn), lambda i, b, j: (b, j)),
                  pl.BlockSpec((S, tn), lambda i, b, j: (b, j))],
        out_specs=pl.BlockSpec((tm, tn), lambda i, b, j: (b * (S // tm) + i, j)),
        out_shape=jax.ShapeDtypeStruct((B * S, D), BF16),
        scratch_shapes=[pltpu.VMEM((tm, S), BF16), pltpu.VMEM((tm, S), BF16)],
        compiler_params=_cparams(("parallel", "arbitrary", "arbitrary")),
        name="fnet_seq",
    )(t1c, t1s, t2c, t2s, hc, hs)


def _conv_kernel(prev_ref, cur_ref, next_ref, w_ref, bdw_ref, lng_ref, lnb_ref, o_ref, buf_ref, cv_ref,
                 *, ts, geo):
    r0 = pl.program_id(0) * ts
    in1 = r0 < geo.T1
    S = jnp.where(in1, geo.S1, geo.S2)
    rel = jnp.where(in1, r0, r0 - geo.T1)
    first = (rel % S) == 0
    last = ((rel + ts) % S) == 0
    H = CONV_HALO
    D = cur_ref.shape[1]
    nslab = D // LANES
    base = H - CONV_PAD
    nload = CONV_WIDTH + CONV_STRIDE - 1

    for c in range(nslab):
        ls = slice(c * LANES, (c + 1) * LANES)
        buf_ref[c, 0:H, :] = jnp.where(first, 0.0, prev_ref[:, ls])
        buf_ref[c, H:H + ts, :] = cur_ref[:, ls]
        buf_ref[c, H + ts:2 * H + ts, :] = jnp.where(last, 0.0, next_ref[:, ls])

    for c in range(nslab):
        ls = slice(c * LANES, (c + 1) * LANES)
        wts = [jnp.broadcast_to(w_ref[k:k + 1, ls], (SUBLANES, LANES)) for k in range(CONV_WIDTH)]
        bias = jnp.broadcast_to(bdw_ref[:, ls], (SUBLANES, LANES))

        def body(n, carry):
            row = n * CONV_ROWS
            accs = [bias] * CONV_STRIDE
            for t in range(nload):
                v = buf_ref[c, pl.ds(row + base + t, SUBLANES, stride=CONV_STRIDE), :]
                for j in range(CONV_STRIDE):
                    k = t - j
                    if 0 <= k < CONV_WIDTH:
                        accs[j] = accs[j] + v * wts[k]
            for j in range(CONV_STRIDE):
                cv_ref[c, pl.ds(row + j, SUBLANES, stride=CONV_STRIDE), :] = accs[j]
            return carry

        lax.fori_loop(0, ts // CONV_ROWS, body, 0)

    cvv = cv_ref[...]
    mu = jnp.sum(jnp.sum(cvv, axis=0), axis=-1, keepdims=True) * (1.0 / D)
    d = cvv - mu[None]
    var = jnp.sum(jnp.sum(d * d, axis=0), axis=-1, keepdims=True) * (1.0 / D)
    inv = lax.rsqrt(var + EPS)
    for c in range(nslab):
        ls = slice(c * LANES, (c + 1) * LANES)
        y = d[c] * inv * lng_ref[:, ls] + lnb_ref[:, ls]
        o_ref[:, ls] = (y * _sigmoid(y)).astype(o_ref.dtype)


def _conv(geo, u, w_dw, lw, b_dw, ln_g, ln_b, *, ts_want=256):
    T, D = u.shape
    ts = _row_tile(geo, ts_want)
    H = CONV_HALO
    assert ts % CONV_ROWS == 0 and ts % H == 0 and D % LANES == 0
    nh = T // H
    vec = lambda: _layer_spec(lw, (1, D), lambda i: (0, 0))
    return pl.pallas_call(
        functools.partial(_conv_kernel, ts=ts, geo=geo),
        grid=(T // ts,),
        in_specs=[pl.BlockSpec((H, D), lambda i: (jnp.maximum(i * (ts // H) - 1, 0), 0)),
                  pl.BlockSpec((ts, D), lambda i: (i, 0)),
                  pl.BlockSpec((H, D), lambda i: (jnp.minimum((i + 1) * (ts // H), nh - 1), 0)),
                  _layer_spec(lw, (CONV_WIDTH, D), lambda i: (0, 0)),
                  vec(), vec(), vec()],
        out_specs=pl.BlockSpec((ts, D), lambda i: (i, 0)),
        out_shape=jax.ShapeDtypeStruct((T, D), BF16),
        scratch_shapes=[pltpu.VMEM((D // LANES, ts + 2 * H, LANES), F32),
                        pltpu.VMEM((D // LANES, ts, LANES), F32)],
        compiler_params=_cparams(("parallel",)),
        name="dwconv_ln_silu",
    )(u, u, u, w_dw, b_dw, ln_g, ln_b)


def _rope(pe, cos, sin):
    return pe * cos + pltpu.roll(pe, ROPE_PAD // 2, 1) * sin


def _mla_down_kernel(x_ref, lnw_ref, sc_ref, sh_ref, w_ref, qn_ref, kvn_ref, cos_ref, sin_ref,
                     cq_ref, ckv_ref, kpe_ref):
    h = _norm_mod(x_ref[...], lnw_ref[...], sc_ref[...], sh_ref[...]).astype(BF16)
    lat = jnp.dot(h, w_ref[...], preferred_element_type=F32)
    qr, kr = MLA_Q_RANK, MLA_KV_RANK
    cq = lat[:, :qr]
    cq_ref[...] = (cq * lax.rsqrt(jnp.mean(cq * cq, axis=-1, keepdims=True) + EPS) * qn_ref[...]).astype(BF16)
    ckv = lat[:, qr:qr + kr]
    ckv_ref[...] = (ckv * lax.rsqrt(jnp.mean(ckv * ckv, axis=-1, keepdims=True) + EPS) * kvn_ref[...]).astype(BF16)
    kpe_ref[...] = _rope(lat[:, qr + kr:], cos_ref[...], sin_ref[...]).astype(BF16)


def _mla_down(geo, x, lnw, layer, mod, w_down_p, q_norm, kv_norm, lw, cos_t, sin_t, *, tm_want=512):
    T, D = x.shape
    N = w_down_p.shape[1]
    tm = _row_tile(geo, tm_want)
    seq = lambda i: _seq_idx(i, tm, geo)
    row = lambda n: pl.BlockSpec((tm, n), lambda i: (i, 0))
    return pl.pallas_call(
        _mla_down_kernel,
        grid=(T // tm,),
        in_specs=[row(D), _layer_spec(layer, (1, D), lambda i: (0, 0)),
                  _mod_spec(layer, SC_M, seq)(D), _mod_spec(layer, SH_M, seq)(D),
                  pl.BlockSpec((D, N), lambda i: (0, 0)),
                  _layer_spec(lw, (1, MLA_Q_RANK), lambda i: (0, 0)),
                  _layer_spec(lw, (1, MLA_KV_RANK), lambda i: (0, 0)),
                  row(ROPE_PAD), row(ROPE_PAD)],
        out_specs=[row(MLA_Q_RANK), row(MLA_KV_RANK), row(ROPE_PAD)],
        out_shape=[jax.ShapeDtypeStruct((T, MLA_Q_RANK), BF16),
                   jax.ShapeDtypeStruct((T, MLA_KV_RANK), BF16),
                   jax.ShapeDtypeStruct((T, ROPE_PAD), BF16)],
        compiler_params=_cparams(("parallel",)),
        name="mla_down",
    )(x, lnw, mod, mod, w_down_p, q_norm, kv_norm, cos_t, sin_t)


def _mla_q_kernel(cq_ref, w_ref, cos_ref, sin_ref, q_ref, *, qscale):
    a = cq_ref[...]
    cos = cos_ref[...] * qscale
    sin = sin_ref[...] * qscale
    for h in range(MLA_HEADS):
        lo = h * QK_PAD
        acc = jnp.dot(a, w_ref[:, lo:lo + QK_PAD], preferred_element_type=F32)
        q_ref[:, lo:lo + MLA_NOPE_DIM] = (acc[:, :MLA_NOPE_DIM] * qscale).astype(BF16)
        q_ref[:, lo + MLA_NOPE_DIM:lo + QK_PAD] = _rope(acc[:, MLA_NOPE_DIM:], cos, sin).astype(BF16)


def _mla_q(geo, cq, w_uq_p, cos_t, sin_t, *, tm_want=512):
    T, R = cq.shape
    N = w_uq_p.shape[1]
    tm = _row_tile(geo, tm_want)
    return pl.pallas_call(
        functools.partial(_mla_q_kernel, qscale=MLA_SCALE * LOG2E),
        grid=(T // tm,),
        in_specs=[pl.BlockSpec((tm, R), lambda i: (i, 0)),
                  pl.BlockSpec((R, N), lambda i: (0, 0)),
                  pl.BlockSpec((tm, ROPE_PAD), lambda i: (i, 0)),
                  pl.BlockSpec((tm, ROPE_PAD), lambda i: (i, 0))],
        out_specs=pl.BlockSpec((tm, N), lambda i: (i, 0)),
        out_shape=jax.ShapeDtypeStruct((T, N), BF16),
        compiler_params=_cparams(("parallel",)),
        name="mla_q_up",
    )(cq, w_uq_p, cos_t, sin_t)


def _mla_kv_kernel(ckv_ref, w_ref, kpe_ref, k_ref, v_ref):
    a = ckv_ref[...]
    kpe = kpe_ref[...]
    hw = MLA_NOPE_DIM + MLA_V_DIM
    for h in range(MLA_HEADS):
        acc = jnp.dot(a, w_ref[:, h * hw:(h + 1) * hw], preferred_element_type=F32)
        lo = h * QK_PAD
        k_ref[:, lo:lo + MLA_NOPE_DIM] = acc[:, :MLA_NOPE_DIM].astype(BF16)
        k_ref[:, lo + MLA_NOPE_DIM:lo + QK_PAD] = kpe
        v_ref[:, h * MLA_V_DIM:(h + 1) * MLA_V_DIM] = acc[:, MLA_NOPE_DIM:].astype(BF16)


def _mla_kv(geo, ckv, w_ukv, lw, kpe, *, tm_want=512):
    T, R = ckv.shape
    N = w_ukv.shape[2]
    tm = _row_tile(geo, tm_want)
    nk = MLA_HEADS * QK_PAD
    nv = MLA_HEADS * MLA_V_DIM
    return pl.pallas_call(
        _mla_kv_kernel,
        grid=(T // tm,),
        in_specs=[pl.BlockSpec((tm, R), lambda i: (i, 0)),
                  _layer_spec(lw, (R, N), lambda i: (0, 0)),
                  pl.BlockSpec((tm, ROPE_PAD), lambda i: (i, 0))],
        out_specs=[pl.BlockSpec((tm, nk), lambda i: (i, 0)),
                   pl.BlockSpec((tm, nv), lambda i: (i, 0))],
        out_shape=[jax.ShapeDtypeStruct((T, nk), BF16),
                   jax.ShapeDtypeStruct((T, nv), BF16)],
        compiler_params=_cparams(("parallel",)),
        name="mla_kv_up",
    )(ckv, w_ukv, kpe)


def _attn_kernel(q_ref, k_ref, v_ref, o_ref, *, tk):
    q = q_ref[...]
    tq = q.shape[0]
    m = jnp.full((tq, 1), -jnp.inf, F32)
    l = jnp.zeros((tq, 1), F32)
    acc = jnp.zeros((tq, MLA_V_DIM), F32)
    for j in range(k_ref.shape[0] // tk):
        ks = slice(j * tk, (j + 1) * tk)
        s = lax.dot_general(q, k_ref[ks, :], (((1,), (1,)), ((), ())), preferred_element_type=F32)
        m_new = jnp.maximum(m, jnp.max(s, axis=-1, keepdims=True))
        alpha = jnp.exp2(m - m_new)
        p = jnp.exp2(s - m_new)
        l = alpha * l + jnp.sum(p, axis=-1, keepdims=True)
        acc = alpha * acc + jnp.dot(p.astype(BF16), v_ref[ks, :], preferred_element_type=F32)
        m = m_new
    o_ref[...] = (acc / l).astype(o_ref.dtype)


def _attn(q, k, v, grp, *, tq_want=512, tk_want=512):
    B, S = grp.B, grp.S
    tq = min(tq_want, S)
    tk = min(tk_want, S)
    assert grp.row_off % S == 0
    ob = grp.row_off // S
    oq = grp.row_off // tq
    nq = S // tq
    return pl.pallas_call(
        functools.partial(_attn_kernel, tk=tk),
        grid=(B, MLA_HEADS, nq),
        in_specs=[pl.BlockSpec((tq, QK_PAD), lambda b, h, i: (oq + b * nq + i, h)),
                  pl.BlockSpec((S, QK_PAD), lambda b, h, i: (ob + b, h)),
                  pl.BlockSpec((S, MLA_V_DIM), lambda b, h, i: (ob + b, h))],
        out_specs=pl.BlockSpec((tq, MLA_V_DIM), lambda b, h, i: (b * nq + i, h)),
        out_shape=jax.ShapeDtypeStruct((B * S, MLA_HEADS * MLA_V_DIM), BF16),
        compiler_params=_cparams(("parallel", "parallel", "arbitrary")),
        name="mla_attn",
    )(q, k, v)


def _final_kernel(x_ref, w_ref, o_ref):
    x = x_ref[...]
    o_ref[...] = x * lax.rsqrt(jnp.mean(x * x, axis=-1, keepdims=True) + EPS) * w_ref[...]


def _final_norm(x, w, grp, *, tm_want=512):
    D = x.shape[1]
    rows = grp.B * grp.S
    tm = min(tm_want, grp.S)
    assert grp.row_off % tm == 0
    ob = grp.row_off // tm
    return pl.pallas_call(
        _final_kernel,
        grid=(rows // tm,),
        in_specs=[pl.BlockSpec((tm, D), lambda i: (ob + i, 0)),
                  pl.BlockSpec((1, D), lambda i: (0, 0))],
        out_specs=pl.BlockSpec((tm, D), lambda i: (i, 0)),
        out_shape=jax.ShapeDtypeStruct((rows, D), F32),
        compiler_params=_cparams(("parallel",)),
        name="final_norm",
    )(x, w)


def _chan_dft_matrix(gd):
    k = jnp.arange(gd, dtype=jnp.int32)
    ang = ((k[:, None] * k[None, :]) % gd).astype(F32) * (2.0 * math.pi / gd)
    nrm = 1.0 / math.sqrt(gd)
    return jnp.concatenate([jnp.cos(ang) * nrm, jnp.sin(ang) * nrm], axis=1).astype(BF16)


def _rope_tables(geo):
    half = MLA_ROPE_DIM // 2
    inv_freq = 1.0 / (ROPE_BASE ** (jnp.arange(0, MLA_ROPE_DIM, 2, dtype=F32) / MLA_ROPE_DIM))
    pos = jnp.concatenate([jnp.tile(jnp.arange(geo.S1, dtype=F32), geo.B1),
                           jnp.tile(jnp.arange(geo.S2, dtype=F32), geo.B2)])
    ang = pos[:, None] * inv_freq[None, :]
    c, s = jnp.cos(ang), jnp.sin(ang)
    z = jnp.zeros_like(c)
    assert ROPE_PAD == 4 * half
    return jnp.concatenate([c, z, c, z], axis=1), jnp.concatenate([-s, z, s, z], axis=1)


def _spread_rope_cols(w):
    half = MLA_ROPE_DIM // 2
    z = jnp.zeros(w.shape[:-1] + (half,), w.dtype)
    return jnp.concatenate([w[..., :half], z, w[..., half:], z], axis=-1)


def kernel(x_prompt, x_sample, c_prompt, c_sample, ln_mix, ln_ffn, w_ada, b_ada, fnet_w_o, fnet_b_o, conv_w_in, conv_b_in, conv_w_dw, conv_b_dw, conv_ln_g, conv_ln_b, conv_w_out, conv_b_out, mla_w_down, mla_q_norm, mla_w_uq, mla_kv_norm, mla_w_ukv, mla_w_o, ffn_w_gate, ffn_w_up, ffn_w_down, final_norm):
    B1, S1, D = x_prompt.shape
    B2, S2, _ = x_sample.shape
    geo = Geo(B1, S1, B2, S2, D)
    g1, g2 = _groups(geo)
    depth = ln_mix.shape[0]
    NB = geo.NB

    x_parts = [x_prompt.reshape(geo.T1, D), x_sample.reshape(geo.T2, D)]

    c_all = jnp.concatenate([c_prompt, c_sample], axis=0)
    c_pad = jnp.pad(c_all, ((0, -NB % BF16_ROWS), (0, 0)))
    mod = _ada(c_pad, w_ada, b_ada)[:, :NB].reshape(depth, NB, 6, 1, D)

    bf = lambda w: w.astype(BF16)
    vec = lambda v: v.reshape(v.shape[0], 1, v.shape[1])
    ln_mix3, ln_ffn3 = vec(ln_mix), vec(ln_ffn)
    w_gate, w_up, w_dn = bf(ffn_w_gate), bf(ffn_w_up), bf(ffn_w_down)
    d_ff = ffn_w_gate.shape[2]

    for i in range(depth):
        kind, j = i % N_MIXERS, i // N_MIXERS
        if kind == 0:
            cs = _chan_dft_matrix(D // FNET_GROUPS)
            f_parts = []
            for n, grp in enumerate((g1, g2)):
                src, off = (x_parts[n], 0) if len(x_parts) == 2 else (x_parts[0], grp.row_off)
                hc, hs = _fnet_chan(src, off, grp, ln_mix3, i, mod, cs)
                f_parts.append(_fnet_seq(hc, hs, grp))
            x = _res(geo, f_parts, bf(fnet_w_o), j, vec(fnet_b_o), j, x_parts, i, mod, G_M, tn_want=D, tm_want=512)
        elif kind == 1:
            x = x_parts[0]
            w_in = bf(conv_w_in)
            u = _dual(geo, x, ln_mix3, i, mod, SC_M, SH_M, w_in, j, 0, w_in, j, D, D, vec(conv_b_in), j,
                      mode="glu", out_dtype=F32)
            u = _conv(geo, u, conv_w_dw, j, vec(conv_b_dw), vec(conv_ln_g), vec(conv_ln_b))
            x = _res(geo, [u], bf(conv_w_out), j, vec(conv_b_out), j, [x], i, mod, G_M, tn_want=D, tm_want=512)
        else:
            x = x_parts[0]
            cos_t, sin_t = _rope_tables(geo)
            qk = MLA_Q_RANK + MLA_KV_RANK
            w_down_p = bf(jnp.concatenate([mla_w_down[j][:, :qk], _spread_rope_cols(mla_w_down[j][:, qk:])], axis=1))
            w_uq = mla_w_uq[j].reshape(MLA_Q_RANK, MLA_HEADS, MLA_QK_DIM)
            w_uq_p = bf(jnp.concatenate([w_uq[..., :MLA_NOPE_DIM], _spread_rope_cols(w_uq[..., MLA_NOPE_DIM:])],
                                        axis=-1).reshape(MLA_Q_RANK, MLA_HEADS * QK_PAD))
            cq, ckv, kpe = _mla_down(geo, x, ln_mix3, i, mod, w_down_p, vec(mla_q_norm), vec(mla_kv_norm), j,
                                     cos_t, sin_t)
            q = _mla_q(geo, cq, w_uq_p, cos_t, sin_t)
            k, v = _mla_kv(geo, ckv, bf(mla_w_ukv), j, kpe)
            o_parts = [_attn(q, k, v, g1), _attn(q, k, v, g2)]
            x = _res(geo, o_parts, bf(mla_w_o), j, None, 0, [x], i, mod, G_M, tn_want=D, tm_want=512)

        u = _dual(geo, x, ln_ffn3, i, mod, SC_F, SH_F, w_gate, i, 0, w_up, i, 0, d_ff, None, 0,
                  mode="swiglu", out_dtype=BF16)
        x = _res(geo, [u], w_dn, i, None, 0, [x], i, mod, G_F)
        x_parts = [x]

    fw = final_norm.reshape(1, D)
    y1 = _final_norm(x, fw, g1).reshape(B1, S1, D)
    y2 = _final_norm(x, fw, g2).reshape(B2, S2, D)
    return (y1, y2)
```

```python
import functools
import math
from typing import NamedTuple

import jax
import jax.numpy as jnp
from jax import lax
from jax.experimental import pallas as pl
from jax.experimental.pallas import tpu as pltpu

F32 = jnp.float32
BF16 = jnp.bfloat16

EPS = 1e-6
N_MIXERS = 3
FNET_GROUPS = 8
CONV_WIDTH = 31
CONV_PAD = CONV_WIDTH // 2
MLA_HEADS = 16
MLA_Q_RANK = 512
MLA_KV_RANK = 512
MLA_NOPE_DIM = 128
MLA_ROPE_DIM = 64
MLA_V_DIM = 128
MLA_QK_DIM = MLA_NOPE_DIM + MLA_ROPE_DIM
MLA_SCALE = 1.0 / math.sqrt(MLA_QK_DIM)
LOG2E = math.log2(math.e)
ROPE_BASE = 10000.0

LANES = 128
SUBLANES = 8
BF16_ROWS = 16
CONV_HALO = 2 * SUBLANES
CONV_STRIDE = 4
CONV_ROWS = CONV_STRIDE * SUBLANES
ROPE_PAD = LANES
QK_PAD = MLA_NOPE_DIM + ROPE_PAD
DFT_RADIX = 4
VMEM_LIMIT = 56 * 1024 * 1024

SH_M, SC_M, G_M, SH_F, SC_F, G_F = range(6)


class Geo(NamedTuple):
    B1: int
    S1: int
    B2: int
    S2: int
    D: int

    @property
    def T1(self):
        return self.B1 * self.S1

    @property
    def T2(self):
        return self.B2 * self.S2

    @property
    def T(self):
        return self.T1 + self.T2

    @property
    def NB(self):
        return self.B1 + self.B2


class Group(NamedTuple):
    B: int
    S: int
    row_off: int
    seq_off: int


def _groups(geo):
    return Group(geo.B1, geo.S1, 0, 0), Group(geo.B2, geo.S2, geo.T1, geo.B1)


def _cparams(sem, vmem=VMEM_LIMIT):
    return pltpu.CompilerParams(dimension_semantics=sem, vmem_limit_bytes=vmem)


def _row_tile(geo, want):
    g = math.gcd(geo.S1, geo.S2)
    t = min(want, g)
    while g % t:
        t //= 2
    return t


def _seq_idx(i, tm, geo):
    n1 = geo.T1 // tm
    return jnp.where(i < n1, i // (geo.S1 // tm), geo.B1 + (i - n1) // (geo.S2 // tm))


def _mod_spec(layer, which, seq_fn):
    def index_map(*g):
        return (layer, seq_fn(*g), which, 0, 0)
    return lambda D: pl.BlockSpec((None, None, None, 1, D), index_map)


def _layer_spec(layer, shape, tail_map, resident=False):
    mode = pl.Buffered(1) if resident else None
    return pl.BlockSpec((None,) + tuple(shape), lambda *g: (layer,) + tuple(tail_map(*g)), pipeline_mode=mode)


def _sigmoid(x):
    return 1.0 / (1.0 + jnp.exp(-x))


def _norm_mod(x, lnw, sc, sh):
    ms = jnp.mean(x * x, axis=-1, keepdims=True)
    y = x * lax.rsqrt(ms + EPS) * lnw
    return y * (1.0 + sc) + sh


def _ada_kernel(c_ref, w_ref, b_ref, o_ref):
    c = c_ref[...]
    a = (c * _sigmoid(c)).astype(BF16)
    o_ref[...] = jnp.dot(a, w_ref[...].astype(BF16), preferred_element_type=F32) + b_ref[...]


def _ada(c_pad, w_ada, b_ada):
    L, D, N = w_ada.shape
    R = c_pad.shape[0]
    tn = min(1024, N)
    return pl.pallas_call(
        _ada_kernel,
        grid=(L, N // tn),
        in_specs=[
            pl.BlockSpec((R, D), lambda l, j: (0, 0)),
            pl.BlockSpec((None, D, tn), lambda l, j: (l, 0, j)),
            pl.BlockSpec((None, 1, tn), lambda l, j: (l, 0, j)),
        ],
        out_specs=pl.BlockSpec((None, R, tn), lambda l, j: (l, 0, j)),
        out_shape=jax.ShapeDtypeStruct((L, R, N), F32),
        compiler_params=_cparams(("parallel", "parallel")),
        name="ada_mod",
    )(c_pad, w_ada, b_ada.reshape(L, 1, N))


def _gate(p1, p2, mode):
    if mode == "swiglu":
        return p1 * _sigmoid(p1) * p2
    return p1 * _sigmoid(p2)


def _gated_kernel(h_ref, w1_ref, w2_ref, *rest, mode, bias):
    if bias:
        b1_ref, b2_ref, o_ref = rest
    else:
        (o_ref,) = rest
    h = h_ref[...]
    p1 = jnp.dot(h, w1_ref[...], preferred_element_type=F32)
    p2 = jnp.dot(h, w2_ref[...], preferred_element_type=F32)
    if bias:
        p1 = p1 + b1_ref[...]
        p2 = p2 + b2_ref[...]
    o_ref[...] = _gate(p1, p2, mode).astype(o_ref.dtype)


def _gated(h, w1, lw1, off1, w2, lw2, off2, n_out, b, lb, *, mode, tm_want=2048, tn_want=512):
    T, D = h.shape
    tm = math.gcd(tm_want, T)
    tn = math.gcd(tn_want, n_out)
    assert off1 % tn == 0 and off2 % tn == 0
    o1, o2 = off1 // tn, off2 // tn
    bias = b is not None
    in_specs = [pl.BlockSpec((tm, D), lambda i, j: (i, 0)),
                _layer_spec(lw1, (D, tn), lambda i, j: (0, j + o1)),
                _layer_spec(lw2, (D, tn), lambda i, j: (0, j + o2))]
    args = [h, w1, w2]
    if bias:
        in_specs += [_layer_spec(lb, (1, tn), lambda i, j: (0, j + o1)),
                     _layer_spec(lb, (1, tn), lambda i, j: (0, j + o2))]
        args += [b, b]
    return pl.pallas_call(
        functools.partial(_gated_kernel, mode=mode, bias=bias),
        grid=(T // tm, n_out // tn),
        in_specs=in_specs,
        out_specs=pl.BlockSpec((tm, tn), lambda i, j: (i, j)),
        out_shape=jax.ShapeDtypeStruct((T, n_out), BF16),
        compiler_params=_cparams(("parallel", "parallel")),
        name="gated_" + mode,
    )(*args)


class Emit(NamedTuple):
    lnw: jax.Array
    layer: int
    which_sc: int
    which_sh: int
    final: bool


def _res_kernel(*refs, bias, na, nx, n1, emit):
    a_refs = refs[:na]
    w_ref = refs[na]
    pos = na + 1
    b_ref = None
    if bias:
        b_ref = refs[pos]
        pos += 1
    x_refs = refs[pos:pos + nx]
    g_ref = refs[pos + nx]
    pos += nx + 1
    if emit == "final":
        lnw_ref, o_ref = refs[pos:pos + 2]
    elif emit == "mod":
        lnw_ref, sc_ref, sh_ref, o_ref, h_ref = refs[pos:pos + 5]
    else:
        o_ref = refs[pos]

    def compute(a_ref, x_ref):
        y = jnp.dot(a_ref[...], w_ref[...], preferred_element_type=F32)
        if bias:
            y = y + b_ref[...]
        xn = x_ref[...] + g_ref[...] * y
        if emit == "final":
            o_ref[...] = xn * lax.rsqrt(jnp.mean(xn * xn, axis=-1, keepdims=True) + EPS) * lnw_ref[...]
            return
        o_ref[...] = xn
        if emit == "mod":
            h_ref[...] = _norm_mod(xn, lnw_ref[...], sc_ref[...], sh_ref[...]).astype(BF16)

    if na == 1 and nx == 1:
        compute(a_refs[0], x_refs[0])
    else:
        i = pl.program_id(0)
        pl.when(i < n1)(lambda: compute(a_refs[0], x_refs[0]))
        pl.when(i >= n1)(lambda: compute(a_refs[-1], x_refs[-1]))


def _res(geo, a_parts, w, lw, b, lb, x_parts, layer, mod, which_g, *, emit=None, rows=None,
         tm_want=1024, tn_want=512):
    K, D = w.shape[1], w.shape[2]
    row_off, nrows = rows if rows is not None else (0, geo.T)
    tm = _row_tile(geo, tm_want)
    tn = min(tn_want, D)
    n1 = geo.T1 // tm
    ob = row_off // tm
    bias = b is not None
    assert row_off % tm == 0 and (rows is None or len(a_parts) == len(x_parts) == 1)

    def part_specs(parts, width, col):
        if len(parts) == 1:
            return [pl.BlockSpec((tm, width), lambda i, j: (ob + i, col(j)))]
        return [pl.BlockSpec((tm, width), lambda i, j: (jnp.minimum(i, n1 - 1), col(j))),
                pl.BlockSpec((tm, width), lambda i, j: (jnp.maximum(i - n1, 0), col(j)))]

    seq = lambda i, j: _seq_idx(ob + i, tm, geo)
    in_specs = part_specs(a_parts, K, lambda j: 0) + [_layer_spec(lw, (K, tn), lambda i, j: (0, j), resident=tn == D)]
    args = list(a_parts) + [w]
    if bias:
        in_specs.append(_layer_spec(lb, (1, tn), lambda i, j: (0, j)))
        args.append(b)
    in_specs += part_specs(x_parts, tn, lambda j: j)
    in_specs.append(pl.BlockSpec((None, None, None, 1, tn), lambda i, j: (layer, seq(i, j), which_g, 0, j)))
    alias = {len(args): 0} if (len(x_parts) == 1 and rows is None and not (emit and emit.final)) else {}
    args += list(x_parts) + [mod]
    out_specs = pl.BlockSpec((tm, tn), lambda i, j: (i, j))
    out_shape = jax.ShapeDtypeStruct((nrows, D), F32)
    kind = None
    if emit is not None:
        assert tn == D
        in_specs.append(_layer_spec(emit.layer, (1, D), lambda i, j: (0, 0)))
        args.append(emit.lnw)
        kind = "final"
        if not emit.final:
            kind = "mod"
            in_specs += [_mod_spec(emit.layer, emit.which_sc, seq)(D), _mod_spec(emit.layer, emit.which_sh, seq)(D)]
            args += [mod, mod]
            out_specs = [out_specs, pl.BlockSpec((tm, D), lambda i, j: (i, 0))]
            out_shape = [out_shape, jax.ShapeDtypeStruct((nrows, D), BF16)]
    return pl.pallas_call(
        functools.partial(_res_kernel, bias=bias, na=len(a_parts), nx=len(x_parts), n1=n1, emit=kind),
        grid=(nrows // tm, D // tn),
        in_specs=in_specs,
        out_specs=out_specs,
        out_shape=out_shape,
        input_output_aliases=alias,
        compiler_params=_cparams(("parallel", "parallel")),
        name="res_mm",
    )(*args)


def _fnet_chan_kernel(*refs, gd, normed):
    xq_refs = refs[:DFT_RADIX]
    if normed:
        cs_ref, tc_ref, ts_ref, vre_ref, vim_ref = refs[DFT_RADIX:]
        hq = [r[...] for r in xq_refs]
    else:
        lnw_ref, sc_ref, sh_ref, cs_ref, tc_ref, ts_ref, vre_ref, vim_ref = refs[DFT_RADIX:]
        lnw, sc, sh = lnw_ref[...], sc_ref[...], sh_ref[...]
        hq = [_norm_mod(r[...], lnw, sc, sh).astype(BF16) for r in xq_refs]
    cs = cs_ref[...]
    for g in range(hq[0].shape[1] // gd):
        p = [jnp.dot(h[:, g * gd:(g + 1) * gd], cs, preferred_element_type=F32) for h in hq]
        for hh in range(gd // LANES):
            re = [pq[:, hh * LANES:(hh + 1) * LANES] for pq in p]
            im = [pq[:, gd + hh * LANES:gd + (hh + 1) * LANES] for pq in p]
            a02p, a02m = re[0] + re[2], re[0] - re[2]
            a13p, a13m = re[1] + re[3], re[1] - re[3]
            b02p, b02m = im[0] + im[2], im[0] - im[2]
            b13p, b13m = im[1] + im[3], im[1] - im[3]
            ure = [a02p + a13p, a02m + b13m, a02p - a13p, a02m - b13m]
            uim = [b02p + b13p, b02m - a13m, b02p - b13p, b02m + a13m]
            ls = slice(g * gd + hh * LANES, g * gd + (hh + 1) * LANES)
            vre_ref[0, :, ls] = ure[0].astype(BF16)
            vim_ref[0, :, ls] = uim[0].astype(BF16)
            for r in range(1, DFT_RADIX):
                c, s = tc_ref[r - 1], ts_ref[r - 1]
                vre_ref[r, :, ls] = (ure[r] * c + uim[r] * s).astype(BF16)
                vim_ref[r, :, ls] = (uim[r] * c - ure[r] * s).astype(BF16)


def _fnet_chan(x, row_off, grp, lnw, layer, mod, cs, *, normed, tm_want=256):
    D = x.shape[1]
    gd = cs.shape[0]
    B, S = grp.B, grp.S
    N = S // DFT_RADIX
    tm = min(tm_want, N)
    assert N % tm == 0 and row_off % tm == 0 and gd % LANES == 0
    s = jnp.arange(N, dtype=jnp.int32)[None, :, None]
    r = jnp.arange(1, DFT_RADIX, dtype=jnp.int32)[:, None, None]
    ang = jnp.broadcast_to(((r * s) % S).astype(F32) * (2.0 * math.pi / S), (DFT_RADIX - 1, N, LANES))
    xq = lambda q: pl.BlockSpec((tm, D), lambda b, i: ((row_off + q * N) // tm + b * (S // tm) + i, 0))
    seq = lambda b, i: grp.seq_off + b
    tw = pl.BlockSpec((DFT_RADIX - 1, tm, LANES), lambda b, i: (0, i, 0))
    out = pl.BlockSpec((None, DFT_RADIX, tm, D), lambda b, i: (b, 0, i, 0))
    in_specs = [xq(q) for q in range(DFT_RADIX)]
    args = [x] * DFT_RADIX
    if not normed:
        in_specs += [_layer_spec(layer, (1, D), lambda b, i: (0, 0)),
                     _mod_spec(layer, SC_M, seq)(D), _mod_spec(layer, SH_M, seq)(D)]
        args += [lnw, mod, mod]
    in_specs += [pl.BlockSpec((gd, 2 * gd), lambda b, i: (0, 0)), tw, tw]
    args += [cs, jnp.cos(ang), jnp.sin(ang)]
    return pl.pallas_call(
        functools.partial(_fnet_chan_kernel, gd=gd, normed=normed),
        grid=(B, N // tm),
        in_specs=in_specs,
        out_specs=[out, out],
        out_shape=[jax.ShapeDtypeStruct((B, DFT_RADIX, N, D), BF16)] * 2,
        compiler_params=_cparams(("parallel", "parallel")),
        name="fnet_chan",
    )(*args)


def _fnet_seq_kernel(c_ref, s_ref, vre_ref, vim_ref, o_ref, slab_ref):
    cw, sw = c_ref[...], s_ref[...]
    tmk = cw.shape[0]
    for r in range(DFT_RADIX):
        y = jnp.dot(cw, vre_ref[r], preferred_element_type=F32)
        y = y + jnp.dot(sw, vim_ref[r], preferred_element_type=F32)
        for c in range(y.shape[1] // LANES):
            slab_ref[c, pl.ds(r, tmk, stride=DFT_RADIX), :] = y[:, c * LANES:(c + 1) * LANES]
    for c in range(o_ref.shape[1] // LANES):
        o_ref[:, c * LANES:(c + 1) * LANES] = slab_ref[c].astype(o_ref.dtype)


def _fnet_seq(vre, vim, grp, *, tmk_want=512, tn_want=512):
    B, S = grp.B, grp.S
    N, D = vre.shape[2], vre.shape[3]
    tmk = min(tmk_want, N)
    tn = min(tn_want, D)
    assert N % tmk == 0 and D % tn == 0 and tn % LANES == 0
    k = jnp.arange(N, dtype=jnp.int32)
    ang = ((k[:, None] * k[None, :]) % N).astype(F32) * (2.0 * math.pi / N)
    nrm = 1.0 / math.sqrt(S)
    cn, sn = (jnp.cos(ang) * nrm).astype(BF16), (jnp.sin(ang) * nrm).astype(BF16)
    wspec = pl.BlockSpec((tmk, N), lambda b, j, kt: (kt, 0))
    vspec = pl.BlockSpec((None, DFT_RADIX, N, tn), lambda b, j, kt: (b, 0, 0, j))
    return pl.pallas_call(
        _fnet_seq_kernel,
        grid=(B, D // tn, N // tmk),
        in_specs=[wspec, wspec, vspec, vspec],
        out_specs=pl.BlockSpec((DFT_RADIX * tmk, tn), lambda b, j, kt: (b * (N // tmk) + kt, j)),
        out_shape=jax.ShapeDtypeStruct((B * S, D), BF16),
        scratch_shapes=[pltpu.VMEM((tn // LANES, DFT_RADIX * tmk, LANES), F32)],
        compiler_params=_cparams(("parallel", "parallel", "arbitrary")),
        name="fnet_seq",
    )(cn, sn, vre, vim)


def _conv_kernel(prev_ref, cur_ref, next_ref, w_ref, bdw_ref, lng_ref, lnb_ref, o_ref, buf_ref, cv_ref,
                 *, ts, geo):
    r0 = pl.program_id(0) * ts
    in1 = r0 < geo.T1
    S = jnp.where(in1, geo.S1, geo.S2)
    rel = jnp.where(in1, r0, r0 - geo.T1)
    first = (rel % S) == 0
    last = ((rel + ts) % S) == 0
    H = CONV_HALO
    D = cur_ref.shape[1]
    nslab = D // LANES
    base = H - CONV_PAD
    nload = CONV_WIDTH + CONV_STRIDE - 1

    for c in range(nslab):
        ls = slice(c * LANES, (c + 1) * LANES)
        buf_ref[c, 0:H, :] = jnp.where(first, 0.0, prev_ref[:, ls].astype(F32))
        buf_ref[c, H:H + ts, :] = cur_ref[:, ls].astype(F32)
        buf_ref[c, H + ts:2 * H + ts, :] = jnp.where(last, 0.0, next_ref[:, ls].astype(F32))

    for c in range(nslab):
        ls = slice(c * LANES, (c + 1) * LANES)
        wts = [jnp.broadcast_to(w_ref[k:k + 1, ls], (SUBLANES, LANES)) for k in range(CONV_WIDTH)]
        bias = jnp.broadcast_to(bdw_ref[:, ls], (SUBLANES, LANES))

        def body(n, carry):
            row = n * CONV_ROWS
            accs = [bias] * CONV_STRIDE
            for t in range(nload):
                v = buf_ref[c, pl.ds(row + base + t, SUBLANES, stride=CONV_STRIDE), :]
                for j in range(CONV_STRIDE):
                    k = t - j
                    if 0 <= k < CONV_WIDTH:
                        accs[j] = accs[j] + v * wts[k]
            for j in range(CONV_STRIDE):
                cv_ref[c, pl.ds(row + j, SUBLANES, stride=CONV_STRIDE), :] = accs[j]
            return carry

        lax.fori_loop(0, ts // CONV_ROWS, body, 0)

    cvv = cv_ref[...]
    mu = jnp.sum(jnp.sum(cvv, axis=0), axis=-1, keepdims=True) * (1.0 / D)
    d = cvv - mu[None]
    var = jnp.sum(jnp.sum(d * d, axis=0), axis=-1, keepdims=True) * (1.0 / D)
    inv = lax.rsqrt(var + EPS)
    for c in range(nslab):
        ls = slice(c * LANES, (c + 1) * LANES)
        y = d[c] * inv * lng_ref[:, ls] + lnb_ref[:, ls]
        o_ref[:, ls] = (y * _sigmoid(y)).astype(o_ref.dtype)


def _conv(geo, u, w_dw, lw, b_dw, ln_g, ln_b, *, ts_want=256):
    T, D = u.shape
    ts = _row_tile(geo, ts_want)
    H = CONV_HALO
    assert ts % CONV_ROWS == 0 and ts % H == 0 and D % LANES == 0
    nh = T // H
    vec = lambda: _layer_spec(lw, (1, D), lambda i: (0, 0))
    return pl.pallas_call(
        functools.partial(_conv_kernel, ts=ts, geo=geo),
        grid=(T // ts,),
        in_specs=[pl.BlockSpec((H, D), lambda i: (jnp.maximum(i * (ts // H) - 1, 0), 0)),
                  pl.BlockSpec((ts, D), lambda i: (i, 0)),
                  pl.BlockSpec((H, D), lambda i: (jnp.minimum((i + 1) * (ts // H), nh - 1), 0)),
                  _layer_spec(lw, (CONV_WIDTH, D), lambda i: (0, 0)),
                  vec(), vec(), vec()],
        out_specs=pl.BlockSpec((ts, D), lambda i: (i, 0)),
        out_shape=jax.ShapeDtypeStruct((T, D), BF16),
        scratch_shapes=[pltpu.VMEM((D // LANES, ts + 2 * H, LANES), F32),
                        pltpu.VMEM((D // LANES, ts, LANES), F32)],
        compiler_params=_cparams(("parallel",)),
        name="dwconv_ln_silu",
    )(u, u, u, w_dw, b_dw, ln_g, ln_b)


def _rope(pe, cos, sin):
    return pe * cos + pltpu.roll(pe, ROPE_PAD // 2, 1) * sin


def _mla_down_kernel(h_ref, w_ref, qn_ref, kvn_ref, cos_ref, sin_ref, cq_ref, ckv_ref, kpe_ref):
    lat = jnp.dot(h_ref[...], w_ref[...], preferred_element_type=F32)
    qr, kr = MLA_Q_RANK, MLA_KV_RANK
    cq = lat[:, :qr]
    cq_ref[...] = (cq * lax.rsqrt(jnp.mean(cq * cq, axis=-1, keepdims=True) + EPS) * qn_ref[...]).astype(BF16)
    ckv = lat[:, qr:qr + kr]
    ckv_ref[...] = (ckv * lax.rsqrt(jnp.mean(ckv * ckv, axis=-1, keepdims=True) + EPS) * kvn_ref[...]).astype(BF16)
    kpe_ref[...] = _rope(lat[:, qr + kr:], cos_ref[...], sin_ref[...]).astype(BF16)


def _mla_down(h, w_down_p, q_norm, kv_norm, lw, cos_t, sin_t, *, tm_want=512):
    T, D = h.shape
    N = w_down_p.shape[1]
    tm = math.gcd(tm_want, T)
    row = lambda n: pl.BlockSpec((tm, n), lambda i: (i, 0))
    return pl.pallas_call(
        _mla_down_kernel,
        grid=(T // tm,),
        in_specs=[row(D),
                  pl.BlockSpec((D, N), lambda i: (0, 0)),
                  _layer_spec(lw, (1, MLA_Q_RANK), lambda i: (0, 0)),
                  _layer_spec(lw, (1, MLA_KV_RANK), lambda i: (0, 0)),
                  row(ROPE_PAD), row(ROPE_PAD)],
        out_specs=[row(MLA_Q_RANK), row(MLA_KV_RANK), row(ROPE_PAD)],
        out_shape=[jax.ShapeDtypeStruct((T, MLA_Q_RANK), BF16),
                   jax.ShapeDtypeStruct((T, MLA_KV_RANK), BF16),
                   jax.ShapeDtypeStruct((T, ROPE_PAD), BF16)],
        compiler_params=_cparams(("parallel",)),
        name="mla_down",
    )(h, w_down_p, q_norm, kv_norm, cos_t, sin_t)


def _mla_q_kernel(cq_ref, w_ref, cos_ref, sin_ref, q_ref, *, qscale):
    a = cq_ref[...]
    cos = cos_ref[...] * qscale
    sin = sin_ref[...] * qscale
    for h in range(MLA_HEADS):
        lo = h * QK_PAD
        acc = jnp.dot(a, w_ref[:, lo:lo + QK_PAD], preferred_element_type=F32)
        q_ref[:, lo:lo + MLA_NOPE_DIM] = (acc[:, :MLA_NOPE_DIM] * qscale).astype(BF16)
        q_ref[:, lo + MLA_NOPE_DIM:lo + QK_PAD] = _rope(acc[:, MLA_NOPE_DIM:], cos, sin).astype(BF16)


def _mla_q(geo, cq, w_uq_p, cos_t, sin_t, *, tm_want=512):
    T, R = cq.shape
    N = w_uq_p.shape[1]
    tm = _row_tile(geo, tm_want)
    return pl.pallas_call(
        functools.partial(_mla_q_kernel, qscale=MLA_SCALE * LOG2E),
        grid=(T // tm,),
        in_specs=[pl.BlockSpec((tm, R), lambda i: (i, 0)),
                  pl.BlockSpec((R, N), lambda i: (0, 0)),
                  pl.BlockSpec((tm, ROPE_PAD), lambda i: (i, 0)),
                  pl.BlockSpec((tm, ROPE_PAD), lambda i: (i, 0))],
        out_specs=pl.BlockSpec((tm, N), lambda i: (i, 0)),
        out_shape=jax.ShapeDtypeStruct((T, N), BF16),
        compiler_params=_cparams(("parallel",)),
        name="mla_q_up",
    )(cq, w_uq_p, cos_t, sin_t)


def _mla_kv_kernel(ckv_ref, w_ref, kpe_ref, k_ref, v_ref):
    a = ckv_ref[...]
    kpe = kpe_ref[...]
    hw = MLA_NOPE_DIM + MLA_V_DIM
    for h in range(MLA_HEADS):
        acc = jnp.dot(a, w_ref[:, h * hw:(h + 1) * hw], preferred_element_type=F32)
        lo = h * QK_PAD
        k_ref[:, lo:lo + MLA_NOPE_DIM] = acc[:, :MLA_NOPE_DIM].astype(BF16)
        k_ref[:, lo + MLA_NOPE_DIM:lo + QK_PAD] = kpe
        v_ref[:, h * MLA_V_DIM:(h + 1) * MLA_V_DIM] = acc[:, MLA_NOPE_DIM:].astype(BF16)


def _mla_kv(geo, ckv, w_ukv, lw, kpe, *, tm_want=512):
    T, R = ckv.shape
    N = w_ukv.shape[2]
    tm = _row_tile(geo, tm_want)
    nk = MLA_HEADS * QK_PAD
    nv = MLA_HEADS * MLA_V_DIM
    return pl.pallas_call(
        _mla_kv_kernel,
        grid=(T // tm,),
        in_specs=[pl.BlockSpec((tm, R), lambda i: (i, 0)),
                  _layer_spec(lw, (R, N), lambda i: (0, 0)),
                  pl.BlockSpec((tm, ROPE_PAD), lambda i: (i, 0))],
        out_specs=[pl.BlockSpec((tm, nk), lambda i: (i, 0)),
                   pl.BlockSpec((tm, nv), lambda i: (i, 0))],
        out_shape=[jax.ShapeDtypeStruct((T, nk), BF16),
                   jax.ShapeDtypeStruct((T, nv), BF16)],
        compiler_params=_cparams(("parallel",)),
        name="mla_kv_up",
    )(ckv, w_ukv, kpe)


def _attn_kernel(q_ref, k_ref, v_ref, o_ref, *, tk):
    q = q_ref[...]
    tq = q.shape[0]
    m = jnp.full((tq, 1), -jnp.inf, F32)
    l = jnp.zeros((tq, 1), F32)
    acc = jnp.zeros((tq, MLA_V_DIM), F32)
    for j in range(k_ref.shape[0] // tk):
        ks = slice(j * tk, (j + 1) * tk)
        s = lax.dot_general(q, k_ref[ks, :], (((1,), (1,)), ((), ())), preferred_element_type=F32)
        m_new = jnp.maximum(m, jnp.max(s, axis=-1, keepdims=True))
        alpha = jnp.exp2(m - m_new)
        p = jnp.exp2(s - m_new)
        l = alpha * l + jnp.sum(p, axis=-1, keepdims=True)
        acc = alpha * acc + jnp.dot(p.astype(BF16), v_ref[ks, :], preferred_element_type=F32)
        m = m_new
    o_ref[...] = (acc / l).astype(o_ref.dtype)


def _attn(q, k, v, grp, *, tq_want=1024, tk_want=1024):
    B, S = grp.B, grp.S
    tq = min(tq_want, S)
    tk = min(tk_want, S)
    assert grp.row_off % S == 0
    ob = grp.row_off // S
    oq = grp.row_off // tq
    nq = S // tq
    return pl.pallas_call(
        functools.partial(_attn_kernel, tk=tk),
        grid=(B, MLA_HEADS, nq),
        in_specs=[pl.BlockSpec((tq, QK_PAD), lambda b, h, i: (oq + b * nq + i, h)),
                  pl.BlockSpec((S, QK_PAD), lambda b, h, i: (ob + b, h)),
                  pl.BlockSpec((S, MLA_V_DIM), lambda b, h, i: (ob + b, h))],
        out_specs=pl.BlockSpec((tq, MLA_V_DIM), lambda b, h, i: (b * nq + i, h)),
        out_shape=jax.ShapeDtypeStruct((B * S, MLA_HEADS * MLA_V_DIM), BF16),
        compiler_params=_cparams(("parallel", "parallel", "arbitrary")),
        name="mla_attn",
    )(q, k, v)


def _chan_dft_matrix(gd):
    k = jnp.arange(gd, dtype=jnp.int32)
    ang = ((k[:, None] * k[None, :]) % gd).astype(F32) * (2.0 * math.pi / gd)
    nrm = 1.0 / math.sqrt(gd)
    return jnp.concatenate([jnp.cos(ang) * nrm, -jnp.sin(ang) * nrm], axis=1).astype(BF16)


def _rope_tables(geo):
    half = MLA_ROPE_DIM // 2
    inv_freq = 1.0 / (ROPE_BASE ** (jnp.arange(0, MLA_ROPE_DIM, 2, dtype=F32) / MLA_ROPE_DIM))
    pos = jnp.concatenate([jnp.tile(jnp.arange(geo.S1, dtype=F32), geo.B1),
                           jnp.tile(jnp.arange(geo.S2, dtype=F32), geo.B2)])
    ang = pos[:, None] * inv_freq[None, :]
    c, s = jnp.cos(ang), jnp.sin(ang)
    z = jnp.zeros_like(c)
    assert ROPE_PAD == 4 * half
    return jnp.concatenate([c, z, c, z], axis=1), jnp.concatenate([-s, z, s, z], axis=1)


def _spread_rope_cols(w):
    half = MLA_ROPE_DIM // 2
    z = jnp.zeros(w.shape[:-1] + (half,), w.dtype)
    return jnp.concatenate([w[..., :half], z, w[..., half:], z], axis=-1)


def kernel(x_prompt, x_sample, c_prompt, c_sample, ln_mix, ln_ffn, w_ada, b_ada, fnet_w_o, fnet_b_o, conv_w_in, conv_b_in, conv_w_dw, conv_b_dw, conv_ln_g, conv_ln_b, conv_w_out, conv_b_out, mla_w_down, mla_q_norm, mla_w_uq, mla_kv_norm, mla_w_ukv, mla_w_o, ffn_w_gate, ffn_w_up, ffn_w_down, final_norm):
    B1, S1, D = x_prompt.shape
    B2, S2, _ = x_sample.shape
    geo = Geo(B1, S1, B2, S2, D)
    g1, g2 = _groups(geo)
    depth = ln_mix.shape[0]
    NB = geo.NB

    x_parts = [x_prompt.reshape(geo.T1, D), x_sample.reshape(geo.T2, D)]

    c_all = jnp.concatenate([c_prompt, c_sample], axis=0)
    c_pad = jnp.pad(c_all, ((0, -NB % BF16_ROWS), (0, 0)))
    mod = _ada(c_pad, w_ada, b_ada)[:, :NB].reshape(depth, NB, 6, 1, D)

    bf = lambda w: w.astype(BF16)
    vec = lambda v: v.reshape(v.shape[0], 1, v.shape[1])
    ln_mix3, ln_ffn3 = vec(ln_mix), vec(ln_ffn)
    w_gate, w_up, w_dn = bf(ffn_w_gate), bf(ffn_w_up), bf(ffn_w_down)
    d_ff = ffn_w_gate.shape[2]
    full = dict(tn_want=D)
    h = None

    for i in range(depth):
        kind, j = i % N_MIXERS, i // N_MIXERS
        close = dict(emit=Emit(ln_ffn3, i, SC_F, SH_F, False), tm_want=512, **full)
        if kind == 0:
            cs = _chan_dft_matrix(D // FNET_GROUPS)
            f_parts = []
            for n, grp in enumerate((g1, g2)):
                if h is None:
                    vre, vim = _fnet_chan(x_parts[n], 0, grp, ln_mix3, i, mod, cs, normed=False)
                else:
                    vre, vim = _fnet_chan(h, grp.row_off, grp, None, i, None, cs, normed=True)
                f_parts.append(_fnet_seq(vre, vim, grp))
            x, h = _res(geo, f_parts, bf(fnet_w_o), j, vec(fnet_b_o), j, x_parts, i, mod, G_M, **close)
        elif kind == 1:
            w_in = bf(conv_w_in)
            u = _gated(h, w_in, j, 0, w_in, j, D, D, vec(conv_b_in), j, mode="glu")
            u = _conv(geo, u, conv_w_dw, j, vec(conv_b_dw), vec(conv_ln_g), vec(conv_ln_b))
            x, h = _res(geo, [u], bf(conv_w_out), j, vec(conv_b_out), j, x_parts, i, mod, G_M, **close)
        else:
            cos_t, sin_t = _rope_tables(geo)
            qk = MLA_Q_RANK + MLA_KV_RANK
            w_down_p = bf(jnp.concatenate([mla_w_down[j][:, :qk], _spread_rope_cols(mla_w_down[j][:, qk:])], axis=1))
            w_uq = mla_w_uq[j].reshape(MLA_Q_RANK, MLA_HEADS, MLA_QK_DIM)
            w_uq_p = bf(jnp.concatenate([w_uq[..., :MLA_NOPE_DIM], _spread_rope_cols(w_uq[..., MLA_NOPE_DIM:])],
                                        axis=-1).reshape(MLA_Q_RANK, MLA_HEADS * QK_PAD))
            cq, ckv, kpe = _mla_down(h, w_down_p, vec(mla_q_norm), vec(mla_kv_norm), j, cos_t, sin_t)
            q = _mla_q(geo, cq, w_uq_p, cos_t, sin_t)
            k, v = _mla_kv(geo, ckv, bf(mla_w_ukv), j, kpe)
            o_parts = [_attn(q, k, v, g1), _attn(q, k, v, g2)]
            x, h = _res(geo, o_parts, bf(mla_w_o), j, None, 0, x_parts, i, mod, G_M, **close)

        u = _gated(h, w_gate, i, 0, w_up, i, 0, d_ff, None, 0, mode="swiglu")
        ffn = (geo, [u], w_dn, i, None, 0, [x], i, mod, G_F)
        if i + 1 < depth:
            x, h = _res(*ffn, emit=Emit(ln_mix3, i + 1, SC_M, SH_M, False), tm_want=256, **full)
            x_parts = [x]

    last = Emit(final_norm.reshape(1, 1, D), 0, 0, 0, True)
    y1, y2 = (_res(*ffn, emit=last, rows=(g.row_off, g.B * g.S), tm_want=256, **full) for g in (g1, g2))
    return (y1.reshape(B1, S1, D), y2.reshape(B2, S2, D))
```

```python
import functools
import math
from typing import NamedTuple

import jax
import jax.numpy as jnp
from jax import lax
from jax.experimental import pallas as pl
from jax.experimental.pallas import tpu as pltpu

F32 = jnp.float32
BF16 = jnp.bfloat16

EPS = 1e-6
N_MIXERS = 3
FNET_GROUPS = 8
CONV_WIDTH = 31
CONV_PAD = CONV_WIDTH // 2
MLA_HEADS = 16
MLA_Q_RANK = 512
MLA_KV_RANK = 512
MLA_NOPE_DIM = 128
MLA_ROPE_DIM = 64
MLA_V_DIM = 128
MLA_QK_DIM = MLA_NOPE_DIM + MLA_ROPE_DIM
MLA_SCALE = 1.0 / math.sqrt(MLA_QK_DIM)
LOG2E = math.log2(math.e)
ROPE_BASE = 10000.0

LANES = 128
SUBLANES = 8
BF16_ROWS = 16
MXU_WIDTH = 256
GATED_ROWS = 1024
CONV_HALO = 2 * SUBLANES
CONV_STRIDE = 4
CONV_ROWS = CONV_STRIDE * SUBLANES
CONV_TILE = 256
ROPE_PAD = LANES
QK_PAD = MLA_NOPE_DIM + ROPE_PAD
DFT_RADIX = 4
VMEM_LIMIT = 56 * 1024 * 1024

SH_M, SC_M, G_M, SH_F, SC_F, G_F = range(6)


class Geo(NamedTuple):
    B1: int
    S1: int
    B2: int
    S2: int
    D: int

    @property
    def T1(self):
        return self.B1 * self.S1

    @property
    def T2(self):
        return self.B2 * self.S2

    @property
    def T(self):
        return self.T1 + self.T2

    @property
    def NB(self):
        return self.B1 + self.B2


class Group(NamedTuple):
    B: int
    S: int
    row_off: int
    seq_off: int


def _groups(geo):
    return Group(geo.B1, geo.S1, 0, 0), Group(geo.B2, geo.S2, geo.T1, geo.B1)


def _cparams(sem, vmem=VMEM_LIMIT):
    return pltpu.CompilerParams(dimension_semantics=sem, vmem_limit_bytes=vmem)


def _row_tile(geo, want):
    g = math.gcd(geo.S1, geo.S2)
    t = min(want, g)
    while g % t:
        t //= 2
    return t


def _seq_idx(i, tm, geo):
    n1 = geo.T1 // tm
    return jnp.where(i < n1, i // (geo.S1 // tm), geo.B1 + (i - n1) // (geo.S2 // tm))


def _mod_spec(layer, which, seq_fn):
    def index_map(*g):
        return (layer, seq_fn(*g), which, 0, 0)
    return lambda D: pl.BlockSpec((None, None, None, 1, D), index_map)


def _layer_spec(layer, shape, tail_map, resident=False):
    mode = pl.Buffered(1) if resident else None
    return pl.BlockSpec((None,) + tuple(shape), lambda *g: (layer,) + tuple(tail_map(*g)), pipeline_mode=mode)


def _sigmoid(x):
    return 1.0 / (1.0 + jnp.exp(-x))


def _norm_mod(x, lnw, sc, sh):
    ms = jnp.mean(x * x, axis=-1, keepdims=True)
    y = x * lax.rsqrt(ms + EPS) * lnw
    return y * (1.0 + sc) + sh


def _ada_kernel(c_ref, w_ref, b_ref, o_ref):
    c = c_ref[...]
    a = (c * _sigmoid(c)).astype(BF16)
    o_ref[...] = jnp.dot(a, w_ref[...].astype(BF16), preferred_element_type=F32) + b_ref[...]


def _ada(c_pad, w_ada, b_ada):
    L, D, N = w_ada.shape
    R = c_pad.shape[0]
    tn = min(1024, N)
    return pl.pallas_call(
        _ada_kernel,
        grid=(L, N // tn),
        in_specs=[
            pl.BlockSpec((R, D), lambda l, j: (0, 0)),
            pl.BlockSpec((None, D, tn), lambda l, j: (l, 0, j)),
            pl.BlockSpec((None, 1, tn), lambda l, j: (l, 0, j)),
        ],
        out_specs=pl.BlockSpec((None, R, tn), lambda l, j: (l, 0, j)),
        out_shape=jax.ShapeDtypeStruct((L, R, N), F32),
        compiler_params=_cparams(("parallel", "parallel")),
        name="ada_mod",
    )(c_pad, w_ada, b_ada.reshape(L, 1, N))


def _gate(p1, p2, mode):
    if mode == "swiglu":
        return p1 * _sigmoid(p1) * p2
    return p1 * _sigmoid(p2)


def _gated_kernel(h_ref, w1_ref, w2_ref, *rest, mode, bias):
    if bias:
        b1_ref, b2_ref, o_ref = rest
    else:
        (o_ref,) = rest
    tm, tn = o_ref.shape
    rc = min(GATED_ROWS, tm)
    for r in range(tm // rc):
        rs = slice(r * rc, (r + 1) * rc)
        h = h_ref[rs, :]
        for c in range(tn // MXU_WIDTH):
            cs = slice(c * MXU_WIDTH, (c + 1) * MXU_WIDTH)
            p1 = jnp.dot(h, w1_ref[:, cs].astype(BF16), preferred_element_type=F32)
            p2 = jnp.dot(h, w2_ref[:, cs].astype(BF16), preferred_element_type=F32)
            if bias:
                p1 = p1 + b1_ref[:, cs]
                p2 = p2 + b2_ref[:, cs]
            o_ref[rs, cs] = _gate(p1, p2, mode).astype(o_ref.dtype)


def _gated(h, w1, lw1, off1, w2, lw2, off2, n_out, b, lb, *, mode, tm_want=2048, tn_want=512):
    T, D = h.shape
    tm = math.gcd(tm_want, T)
    tn = math.gcd(tn_want, n_out)
    assert off1 % tn == 0 and off2 % tn == 0 and tn % MXU_WIDTH == 0
    o1, o2 = off1 // tn, off2 // tn
    bias = b is not None
    in_specs = [pl.BlockSpec((tm, D), lambda i, j: (i, 0)),
                _layer_spec(lw1, (D, tn), lambda i, j: (0, j + o1)),
                _layer_spec(lw2, (D, tn), lambda i, j: (0, j + o2))]
    args = [h, w1, w2]
    if bias:
        in_specs += [_layer_spec(lb, (1, tn), lambda i, j: (0, j + o1)),
                     _layer_spec(lb, (1, tn), lambda i, j: (0, j + o2))]
        args += [b, b]
    return pl.pallas_call(
        functools.partial(_gated_kernel, mode=mode, bias=bias),
        grid=(T // tm, n_out // tn),
        in_specs=in_specs,
        out_specs=pl.BlockSpec((tm, tn), lambda i, j: (i, j)),
        out_shape=jax.ShapeDtypeStruct((T, n_out), BF16),
        compiler_params=_cparams(("parallel", "parallel")),
        name="gated_" + mode,
    )(*args)


class Emit(NamedTuple):
    lnw: jax.Array
    layer: int
    which_sc: int
    which_sh: int
    final: bool


def _res_kernel(*refs, bias, na, nx, n1, emit):
    a_refs = refs[:na]
    w_ref = refs[na]
    pos = na + 1
    b_ref = None
    if bias:
        b_ref = refs[pos]
        pos += 1
    x_refs = refs[pos:pos + nx]
    g_ref = refs[pos + nx]
    pos += nx + 1
    if emit == "final":
        lnw_ref, o_ref = refs[pos:pos + 2]
    elif emit == "mod":
        lnw_ref, sc_ref, sh_ref, o_ref, h_ref = refs[pos:pos + 5]
    else:
        o_ref = refs[pos]

    def compute(a_ref, x_ref):
        y = jnp.dot(a_ref[...], w_ref[...], preferred_element_type=F32)
        if bias:
            y = y + b_ref[...]
        xn = x_ref[...] + g_ref[...] * y
        if emit == "final":
            o_ref[...] = xn * lax.rsqrt(jnp.mean(xn * xn, axis=-1, keepdims=True) + EPS) * lnw_ref[...]
            return
        o_ref[...] = xn
        if emit == "mod":
            h_ref[...] = _norm_mod(xn, lnw_ref[...], sc_ref[...], sh_ref[...]).astype(BF16)

    if na == 1 and nx == 1:
        compute(a_refs[0], x_refs[0])
    else:
        i = pl.program_id(0)
        pl.when(i < n1)(lambda: compute(a_refs[0], x_refs[0]))
        pl.when(i >= n1)(lambda: compute(a_refs[-1], x_refs[-1]))


def _res(geo, a_parts, w, lw, b, lb, x_parts, layer, mod, which_g, *, emit=None, rows=None,
         tm_want=1024, tn_want=512):
    K, D = w.shape[1], w.shape[2]
    row_off, nrows = rows if rows is not None else (0, geo.T)
    tm = _row_tile(geo, tm_want)
    tn = min(tn_want, D)
    n1 = geo.T1 // tm
    ob = row_off // tm
    bias = b is not None
    assert row_off % tm == 0 and (rows is None or len(a_parts) == len(x_parts) == 1)

    def part_specs(parts, width, col):
        if len(parts) == 1:
            return [pl.BlockSpec((tm, width), lambda i, j: (ob + i, col(j)))]
        return [pl.BlockSpec((tm, width), lambda i, j: (jnp.minimum(i, n1 - 1), col(j))),
                pl.BlockSpec((tm, width), lambda i, j: (jnp.maximum(i - n1, 0), col(j)))]

    seq = lambda i, j: _seq_idx(ob + i, tm, geo)
    in_specs = part_specs(a_parts, K, lambda j: 0) + [_layer_spec(lw, (K, tn), lambda i, j: (0, j), resident=tn == D)]
    args = list(a_parts) + [w]
    if bias:
        in_specs.append(_layer_spec(lb, (1, tn), lambda i, j: (0, j)))
        args.append(b)
    in_specs += part_specs(x_parts, tn, lambda j: j)
    in_specs.append(pl.BlockSpec((None, None, None, 1, tn), lambda i, j: (layer, seq(i, j), which_g, 0, j)))
    alias = {len(args): 0} if (len(x_parts) == 1 and rows is None and not (emit and emit.final)) else {}
    args += list(x_parts) + [mod]
    out_specs = pl.BlockSpec((tm, tn), lambda i, j: (i, j))
    out_shape = jax.ShapeDtypeStruct((nrows, D), F32)
    kind = None
    if emit is not None:
        assert tn == D
        in_specs.append(_layer_spec(emit.layer, (1, D), lambda i, j: (0, 0)))
        args.append(emit.lnw)
        kind = "final"
        if not emit.final:
            kind = "mod"
            in_specs += [_mod_spec(emit.layer, emit.which_sc, seq)(D), _mod_spec(emit.layer, emit.which_sh, seq)(D)]
            args += [mod, mod]
            out_specs = [out_specs, pl.BlockSpec((tm, D), lambda i, j: (i, 0))]
            out_shape = [out_shape, jax.ShapeDtypeStruct((nrows, D), BF16)]
    return pl.pallas_call(
        functools.partial(_res_kernel, bias=bias, na=len(a_parts), nx=len(x_parts), n1=n1, emit=kind),
        grid=(nrows // tm, D // tn),
        in_specs=in_specs,
        out_specs=out_specs,
        out_shape=out_shape,
        input_output_aliases=alias,
        compiler_params=_cparams(("parallel", "parallel")),
        name="res_mm",
    )(*args)


def _fnet_chan_kernel(*refs, gd, normed):
    xq_refs = refs[:DFT_RADIX]
    if normed:
        cs_ref, tc_ref, ts_ref, vre_ref, vim_ref = refs[DFT_RADIX:]
        hq = [r[...] for r in xq_refs]
    else:
        lnw_ref, sc_ref, sh_ref, cs_ref, tc_ref, ts_ref, vre_ref, vim_ref = refs[DFT_RADIX:]
        lnw, sc, sh = lnw_ref[...], sc_ref[...], sh_ref[...]
        hq = [_norm_mod(r[...], lnw, sc, sh).astype(BF16) for r in xq_refs]
    cs = cs_ref[...]
    for g in range(hq[0].shape[1] // gd):
        p = [jnp.dot(h[:, g * gd:(g + 1) * gd], cs, preferred_element_type=F32) for h in hq]
        for hh in range(gd // LANES):
            re = [pq[:, hh * LANES:(hh + 1) * LANES] for pq in p]
            im = [pq[:, gd + hh * LANES:gd + (hh + 1) * LANES] for pq in p]
            a02p, a02m = re[0] + re[2], re[0] - re[2]
            a13p, a13m = re[1] + re[3], re[1] - re[3]
            b02p, b02m = im[0] + im[2], im[0] - im[2]
            b13p, b13m = im[1] + im[3], im[1] - im[3]
            ure = [a02p + a13p, a02m + b13m, a02p - a13p, a02m - b13m]
            uim = [b02p + b13p, b02m - a13m, b02p - b13p, b02m + a13m]
            ls = slice(g * gd + hh * LANES, g * gd + (hh + 1) * LANES)
            vre_ref[0, :, ls] = ure[0].astype(BF16)
            vim_ref[0, :, ls] = uim[0].astype(BF16)
            for r in range(1, DFT_RADIX):
                c, s = tc_ref[r - 1], ts_ref[r - 1]
                vre_ref[r, :, ls] = (ure[r] * c + uim[r] * s).astype(BF16)
                vim_ref[r, :, ls] = (uim[r] * c - ure[r] * s).astype(BF16)


def _fnet_chan(x, row_off, grp, lnw, layer, mod, cs, *, normed, tm_want=256):
    D = x.shape[1]
    gd = cs.shape[0]
    B, S = grp.B, grp.S
    N = S // DFT_RADIX
    tm = min(tm_want, N)
    assert N % tm == 0 and row_off % tm == 0 and gd % LANES == 0
    s = jnp.arange(N, dtype=jnp.int32)[None, :, None]
    r = jnp.arange(1, DFT_RADIX, dtype=jnp.int32)[:, None, None]
    ang = jnp.broadcast_to(((r * s) % S).astype(F32) * (2.0 * math.pi / S), (DFT_RADIX - 1, N, LANES))
    xq = lambda q: pl.BlockSpec((tm, D), lambda b, i: ((row_off + q * N) // tm + b * (S // tm) + i, 0))
    seq = lambda b, i: grp.seq_off + b
    tw = pl.BlockSpec((DFT_RADIX - 1, tm, LANES), lambda b, i: (0, i, 0))
    out = pl.BlockSpec((None, DFT_RADIX, tm, D), lambda b, i: (b, 0, i, 0))
    in_specs = [xq(q) for q in range(DFT_RADIX)]
    args = [x] * DFT_RADIX
    if not normed:
        in_specs += [_layer_spec(layer, (1, D), lambda b, i: (0, 0)),
                     _mod_spec(layer, SC_M, seq)(D), _mod_spec(layer, SH_M, seq)(D)]
        args += [lnw, mod, mod]
    in_specs += [pl.BlockSpec((gd, 2 * gd), lambda b, i: (0, 0)), tw, tw]
    args += [cs, jnp.cos(ang), jnp.sin(ang)]
    return pl.pallas_call(
        functools.partial(_fnet_chan_kernel, gd=gd, normed=normed),
        grid=(B, N // tm),
        in_specs=in_specs,
        out_specs=[out, out],
        out_shape=[jax.ShapeDtypeStruct((B, DFT_RADIX, N, D), BF16)] * 2,
        compiler_params=_cparams(("parallel", "parallel")),
        name="fnet_chan",
    )(*args)


def _fnet_seq_kernel(c_ref, s_ref, vre_ref, vim_ref, o_ref, slab_ref):
    cw, sw = c_ref[...], s_ref[...]
    tmk = cw.shape[0]
    for r in range(DFT_RADIX):
        y = jnp.dot(cw, vre_ref[r], preferred_element_type=F32)
        y = y + jnp.dot(sw, vim_ref[r], preferred_element_type=F32)
        for c in range(y.shape[1] // LANES):
            slab_ref[c, pl.ds(r, tmk, stride=DFT_RADIX), :] = y[:, c * LANES:(c + 1) * LANES]
    for c in range(o_ref.shape[1] // LANES):
        o_ref[:, c * LANES:(c + 1) * LANES] = slab_ref[c].astype(o_ref.dtype)


def _fnet_seq(vre, vim, grp, *, tmk_want=512, tn_want=512):
    B, S = grp.B, grp.S
    N, D = vre.shape[2], vre.shape[3]
    tmk = min(tmk_want, N)
    tn = min(tn_want, D)
    assert N % tmk == 0 and D % tn == 0 and tn % LANES == 0
    k = jnp.arange(N, dtype=jnp.int32)
    ang = ((k[:, None] * k[None, :]) % N).astype(F32) * (2.0 * math.pi / N)
    nrm = 1.0 / math.sqrt(S)
    cn, sn = (jnp.cos(ang) * nrm).astype(BF16), (jnp.sin(ang) * nrm).astype(BF16)
    wspec = pl.BlockSpec((tmk, N), lambda b, j, kt: (kt, 0))
    vspec = pl.BlockSpec((None, DFT_RADIX, N, tn), lambda b, j, kt: (b, 0, 0, j))
    return pl.pallas_call(
        _fnet_seq_kernel,
        grid=(B, D // tn, N // tmk),
        in_specs=[wspec, wspec, vspec, vspec],
        out_specs=pl.BlockSpec((DFT_RADIX * tmk, tn), lambda b, j, kt: (b * (N // tmk) + kt, j)),
        out_shape=jax.ShapeDtypeStruct((B * S, D), BF16),
        scratch_shapes=[pltpu.VMEM((tn // LANES, DFT_RADIX * tmk, LANES), F32)],
        compiler_params=_cparams(("parallel", "parallel", "arbitrary")),
        name="fnet_seq",
    )(cn, sn, vre, vim)


def _conv_kernel(prev_ref, cur_ref, next_ref, w_ref, bdw_ref, lng_ref, lnb_ref, o_ref, buf_ref, cv_ref,
                 *, ts, geo):
    r0 = pl.program_id(0) * ts
    in1 = r0 < geo.T1
    S = jnp.where(in1, geo.S1, geo.S2)
    rel = jnp.where(in1, r0, r0 - geo.T1)
    first = (rel % S) == 0
    last = ((rel + ts) % S) == 0
    H = CONV_HALO
    D = cur_ref.shape[1]
    nslab = D // LANES
    base = H - CONV_PAD
    nload = CONV_WIDTH + CONV_STRIDE - 1

    for c in range(nslab):
        ls = slice(c * LANES, (c + 1) * LANES)
        buf_ref[c, 0:H, :] = jnp.where(first, 0.0, prev_ref[:, ls].astype(F32))
        buf_ref[c, H:H + ts, :] = cur_ref[:, ls].astype(F32)
        buf_ref[c, H + ts:2 * H + ts, :] = jnp.where(last, 0.0, next_ref[:, ls].astype(F32))

    for c in range(nslab):
        ls = slice(c * LANES, (c + 1) * LANES)
        wts = [jnp.broadcast_to(w_ref[k:k + 1, ls], (SUBLANES, LANES)) for k in range(CONV_WIDTH)]
        bias = jnp.broadcast_to(bdw_ref[:, ls], (SUBLANES, LANES))

        def body(n, carry):
            row = n * CONV_ROWS
            accs = [bias] * CONV_STRIDE
            for t in range(nload):
                v = buf_ref[c, pl.ds(row + base + t, SUBLANES, stride=CONV_STRIDE), :]
                for j in range(CONV_STRIDE):
                    k = t - j
                    if 0 <= k < CONV_WIDTH:
                        accs[j] = accs[j] + v * wts[k]
            for j in range(CONV_STRIDE):
                cv_ref[c, pl.ds(row + j, SUBLANES, stride=CONV_STRIDE), :] = accs[j]
            return carry

        lax.fori_loop(0, ts // CONV_ROWS, body, 0, unroll=True)

    cvv = cv_ref[...]
    mu = jnp.sum(jnp.sum(cvv, axis=0), axis=-1, keepdims=True) * (1.0 / D)
    d = cvv - mu[None]
    var = jnp.sum(jnp.sum(d * d, axis=0), axis=-1, keepdims=True) * (1.0 / D)
    inv = lax.rsqrt(var + EPS)
    for c in range(nslab):
        ls = slice(c * LANES, (c + 1) * LANES)
        y = d[c] * inv * lng_ref[:, ls] + lnb_ref[:, ls]
        o_ref[:, ls] = (y * _sigmoid(y)).astype(o_ref.dtype)


def _conv(geo, u, w_dw, lw, b_dw, ln_g, ln_b, *, ts_want=CONV_TILE):
    T, D = u.shape
    ts = _row_tile(geo, ts_want)
    H = CONV_HALO
    assert ts % CONV_ROWS == 0 and ts % H == 0 and D % LANES == 0
    nh = T // H
    vec = lambda: _layer_spec(lw, (1, D), lambda i: (0, 0))
    return pl.pallas_call(
        functools.partial(_conv_kernel, ts=ts, geo=geo),
        grid=(T // ts,),
        in_specs=[pl.BlockSpec((H, D), lambda i: (jnp.maximum(i * (ts // H) - 1, 0), 0)),
                  pl.BlockSpec((ts, D), lambda i: (i, 0)),
                  pl.BlockSpec((H, D), lambda i: (jnp.minimum((i + 1) * (ts // H), nh - 1), 0)),
                  _layer_spec(lw, (CONV_WIDTH, D), lambda i: (0, 0)),
                  vec(), vec(), vec()],
        out_specs=pl.BlockSpec((ts, D), lambda i: (i, 0)),
        out_shape=jax.ShapeDtypeStruct((T, D), BF16),
        scratch_shapes=[pltpu.VMEM((D // LANES, ts + 2 * H, LANES), F32),
                        pltpu.VMEM((D // LANES, ts, LANES), F32)],
        compiler_params=_cparams(("parallel",)),
        name="dwconv_ln_silu",
    )(u, u, u, w_dw, b_dw, ln_g, ln_b)


def _rope(pe, cos, sin):
    return pe * cos + pltpu.roll(pe, ROPE_PAD // 2, 1) * sin


def _mla_down_kernel(h_ref, w_ref, qn_ref, kvn_ref, cos_ref, sin_ref, cq_ref, ckv_ref, kpe_ref):
    h = h_ref[...]
    qr, kr = MLA_Q_RANK, MLA_KV_RANK
    cq = jnp.dot(h, w_ref[:, :qr], preferred_element_type=F32)
    cq_ref[...] = (cq * lax.rsqrt(jnp.mean(cq * cq, axis=-1, keepdims=True) + EPS) * qn_ref[...]).astype(BF16)
    ckv = jnp.dot(h, w_ref[:, qr:qr + kr], preferred_element_type=F32)
    ckv_ref[...] = (ckv * lax.rsqrt(jnp.mean(ckv * ckv, axis=-1, keepdims=True) + EPS) * kvn_ref[...]).astype(BF16)
    pe = jnp.dot(h, w_ref[:, qr + kr:], preferred_element_type=F32)
    kpe_ref[...] = _rope(pe, cos_ref[...], sin_ref[...]).astype(BF16)


def _mla_down(h, w_down_p, q_norm, kv_norm, lw, cos_t, sin_t, *, tm_want=512):
    T, D = h.shape
    N = w_down_p.shape[1]
    tm = math.gcd(tm_want, T)
    row = lambda n: pl.BlockSpec((tm, n), lambda i: (i, 0))
    return pl.pallas_call(
        _mla_down_kernel,
        grid=(T // tm,),
        in_specs=[row(D),
                  pl.BlockSpec((D, N), lambda i: (0, 0)),
                  _layer_spec(lw, (1, MLA_Q_RANK), lambda i: (0, 0)),
                  _layer_spec(lw, (1, MLA_KV_RANK), lambda i: (0, 0)),
                  row(ROPE_PAD), row(ROPE_PAD)],
        out_specs=[row(MLA_Q_RANK), row(MLA_KV_RANK), row(ROPE_PAD)],
        out_shape=[jax.ShapeDtypeStruct((T, MLA_Q_RANK), BF16),
                   jax.ShapeDtypeStruct((T, MLA_KV_RANK), BF16),
                   jax.ShapeDtypeStruct((T, ROPE_PAD), BF16)],
        compiler_params=_cparams(("parallel",)),
        name="mla_down",
    )(h, w_down_p, q_norm, kv_norm, cos_t, sin_t)


def _mla_q_kernel(cq_ref, w_ref, cos_ref, sin_ref, q_ref, *, qscale):
    a = cq_ref[...]
    cos = cos_ref[...] * qscale
    sin = sin_ref[...] * qscale
    for h in range(MLA_HEADS):
        lo = h * QK_PAD
        acc = jnp.dot(a, w_ref[:, lo:lo + QK_PAD], preferred_element_type=F32)
        q_ref[:, lo:lo + MLA_NOPE_DIM] = (acc[:, :MLA_NOPE_DIM] * qscale).astype(BF16)
        q_ref[:, lo + MLA_NOPE_DIM:lo + QK_PAD] = _rope(acc[:, MLA_NOPE_DIM:], cos, sin).astype(BF16)


def _mla_q(geo, cq, w_uq_p, cos_t, sin_t, *, tm_want=512):
    T, R = cq.shape
    N = w_uq_p.shape[1]
    tm = _row_tile(geo, tm_want)
    return pl.pallas_call(
        functools.partial(_mla_q_kernel, qscale=MLA_SCALE * LOG2E),
        grid=(T // tm,),
        in_specs=[pl.BlockSpec((tm, R), lambda i: (i, 0)),
                  pl.BlockSpec((R, N), lambda i: (0, 0)),
                  pl.BlockSpec((tm, ROPE_PAD), lambda i: (i, 0)),
                  pl.BlockSpec((tm, ROPE_PAD), lambda i: (i, 0))],
        out_specs=pl.BlockSpec((tm, N), lambda i: (i, 0)),
        out_shape=jax.ShapeDtypeStruct((T, N), BF16),
        compiler_params=_cparams(("parallel",)),
        name="mla_q_up",
    )(cq, w_uq_p, cos_t, sin_t)


def _mla_kv_kernel(ckv_ref, w_ref, kpe_ref, k_ref, v_ref):
    a = ckv_ref[...]
    kpe = kpe_ref[...]
    hw = MLA_NOPE_DIM + MLA_V_DIM
    for h in range(MLA_HEADS):
        acc = jnp.dot(a, w_ref[:, h * hw:(h + 1) * hw], preferred_element_type=F32)
        lo = h * QK_PAD
        k_ref[:, lo:lo + MLA_NOPE_DIM] = acc[:, :MLA_NOPE_DIM].astype(BF16)
        k_ref[:, lo + MLA_NOPE_DIM:lo + QK_PAD] = kpe
        v_ref[:, h * MLA_V_DIM:(h + 1) * MLA_V_DIM] = acc[:, MLA_NOPE_DIM:].astype(BF16)


def _mla_kv(geo, ckv, w_ukv, lw, kpe, *, tm_want=512):
    T, R = ckv.shape
    N = w_ukv.shape[2]
    tm = _row_tile(geo, tm_want)
    nk = MLA_HEADS * QK_PAD
    nv = MLA_HEADS * MLA_V_DIM
    return pl.pallas_call(
        _mla_kv_kernel,
        grid=(T // tm,),
        in_specs=[pl.BlockSpec((tm, R), lambda i: (i, 0)),
                  _layer_spec(lw, (R, N), lambda i: (0, 0)),
                  pl.BlockSpec((tm, ROPE_PAD), lambda i: (i, 0))],
        out_specs=[pl.BlockSpec((tm, nk), lambda i: (i, 0)),
                   pl.BlockSpec((tm, nv), lambda i: (i, 0))],
        out_shape=[jax.ShapeDtypeStruct((T, nk), BF16),
                   jax.ShapeDtypeStruct((T, nv), BF16)],
        compiler_params=_cparams(("parallel",)),
        name="mla_kv_up",
    )(ckv, w_ukv, kpe)


def _attn_kernel(q_ref, k_ref, v_ref, o_ref, *, tk):
    q = q_ref[...]
    tq = q.shape[0]
    m = jnp.full((tq, 1), -jnp.inf, F32)
    l = jnp.zeros((tq, 1), F32)
    acc = jnp.zeros((tq, MLA_V_DIM), F32)
    for j in range(k_ref.shape[0] // tk):
        ks = slice(j * tk, (j + 1) * tk)
        s = lax.dot_general(q, k_ref[ks, :], (((1,), (1,)), ((), ())), preferred_element_type=F32)
        m_new = jnp.maximum(m, jnp.max(s, axis=-1, keepdims=True))
        alpha = jnp.exp2(m - m_new)
        p = jnp.exp2(s - m_new)
        l = alpha * l + jnp.sum(p, axis=-1, keepdims=True)
        acc = alpha * acc + jnp.dot(p.astype(BF16), v_ref[ks, :], preferred_element_type=F32)
        m = m_new
    o_ref[...] = (acc / l).astype(o_ref.dtype)


def _attn(q, k, v, grp, *, tq_want=1024, tk_want=1024):
    B, S = grp.B, grp.S
    tq = min(tq_want, S)
    tk = min(tk_want, S)
    assert grp.row_off % S == 0
    ob = grp.row_off // S
    oq = grp.row_off // tq
    nq = S // tq
    return pl.pallas_call(
        functools.partial(_attn_kernel, tk=tk),
        grid=(B, MLA_HEADS, nq),
        in_specs=[pl.BlockSpec((tq, QK_PAD), lambda b, h, i: (oq + b * nq + i, h)),
                  pl.BlockSpec((S, QK_PAD), lambda b, h, i: (ob + b, h)),
                  pl.BlockSpec((S, MLA_V_DIM), lambda b, h, i: (ob + b, h))],
        out_specs=pl.BlockSpec((tq, MLA_V_DIM), lambda b, h, i: (b * nq + i, h)),
        out_shape=jax.ShapeDtypeStruct((B * S, MLA_HEADS * MLA_V_DIM), BF16),
        compiler_params=_cparams(("parallel", "parallel", "arbitrary")),
        name="mla_attn",
    )(q, k, v)


def _chan_dft_matrix(gd):
    k = jnp.arange(gd, dtype=jnp.int32)
    ang = ((k[:, None] * k[None, :]) % gd).astype(F32) * (2.0 * math.pi / gd)
    nrm = 1.0 / math.sqrt(gd)
    return jnp.concatenate([jnp.cos(ang) * nrm, -jnp.sin(ang) * nrm], axis=1).astype(BF16)


def _rope_tables(geo):
    half = MLA_ROPE_DIM // 2
    inv_freq = 1.0 / (ROPE_BASE ** (jnp.arange(0, MLA_ROPE_DIM, 2, dtype=F32) / MLA_ROPE_DIM))
    pos = jnp.concatenate([jnp.tile(jnp.arange(geo.S1, dtype=F32), geo.B1),
                           jnp.tile(jnp.arange(geo.S2, dtype=F32), geo.B2)])
    ang = pos[:, None] * inv_freq[None, :]
    c, s = jnp.cos(ang), jnp.sin(ang)
    z = jnp.zeros_like(c)
    assert ROPE_PAD == 4 * half
    return jnp.concatenate([c, z, c, z], axis=1), jnp.concatenate([-s, z, s, z], axis=1)


def _spread_rope_cols(w):
    half = MLA_ROPE_DIM // 2
    z = jnp.zeros(w.shape[:-1] + (half,), w.dtype)
    return jnp.concatenate([w[..., :half], z, w[..., half:], z], axis=-1)


def kernel(x_prompt, x_sample, c_prompt, c_sample, ln_mix, ln_ffn, w_ada, b_ada, fnet_w_o, fnet_b_o, conv_w_in, conv_b_in, conv_w_dw, conv_b_dw, conv_ln_g, conv_ln_b, conv_w_out, conv_b_out, mla_w_down, mla_q_norm, mla_w_uq, mla_kv_norm, mla_w_ukv, mla_w_o, ffn_w_gate, ffn_w_up, ffn_w_down, final_norm):
    B1, S1, D = x_prompt.shape
    B2, S2, _ = x_sample.shape
    geo = Geo(B1, S1, B2, S2, D)
    g1, g2 = _groups(geo)
    depth = ln_mix.shape[0]
    NB = geo.NB

    x_parts = [x_prompt.reshape(geo.T1, D), x_sample.reshape(geo.T2, D)]

    c_all = jnp.concatenate([c_prompt, c_sample], axis=0)
    c_pad = jnp.pad(c_all, ((0, -NB % BF16_ROWS), (0, 0)))
    mod = _ada(c_pad, w_ada, b_ada)[:, :NB].reshape(depth, NB, 6, 1, D)

    bf = lambda w: w.astype(BF16)
    vec = lambda v: v.reshape(v.shape[0], 1, v.shape[1])
    ln_mix3, ln_ffn3 = vec(ln_mix), vec(ln_ffn)
    w_gate, w_up, w_dn = ffn_w_gate, ffn_w_up, bf(ffn_w_down)
    d_ff = ffn_w_gate.shape[2]
    full = dict(tn_want=D)
    h = None

    for i in range(depth):
        kind, j = i % N_MIXERS, i // N_MIXERS
        close = dict(emit=Emit(ln_ffn3, i, SC_F, SH_F, False), tm_want=512, **full)
        if kind == 0:
            cs = _chan_dft_matrix(D // FNET_GROUPS)
            f_parts = []
            for n, grp in enumerate((g1, g2)):
                if h is None:
                    vre, vim = _fnet_chan(x_parts[n], 0, grp, ln_mix3, i, mod, cs, normed=False)
                else:
                    vre, vim = _fnet_chan(h, grp.row_off, grp, None, i, None, cs, normed=True)
                f_parts.append(_fnet_seq(vre, vim, grp))
            x, h = _res(geo, f_parts, bf(fnet_w_o), j, vec(fnet_b_o), j, x_parts, i, mod, G_M, **close)
        elif kind == 1:
            u = _gated(h, conv_w_in, j, 0, conv_w_in, j, D, D, vec(conv_b_in), j, mode="glu")
            u = _conv(geo, u, conv_w_dw, j, vec(conv_b_dw), vec(conv_ln_g), vec(conv_ln_b))
            x, h = _res(geo, [u], bf(conv_w_out), j, vec(conv_b_out), j, x_parts, i, mod, G_M, **close)
        else:
            cos_t, sin_t = _rope_tables(geo)
            qk = MLA_Q_RANK + MLA_KV_RANK
            w_down_p = bf(jnp.concatenate([mla_w_down[j][:, :qk], _spread_rope_cols(mla_w_down[j][:, qk:])], axis=1))
            w_uq = mla_w_uq[j].reshape(MLA_Q_RANK, MLA_HEADS, MLA_QK_DIM)
            w_uq_p = bf(jnp.concatenate([w_uq[..., :MLA_NOPE_DIM], _spread_rope_cols(w_uq[..., MLA_NOPE_DIM:])],
                                        axis=-1).reshape(MLA_Q_RANK, MLA_HEADS * QK_PAD))
            cq, ckv, kpe = _mla_down(h, w_down_p, vec(mla_q_norm), vec(mla_kv_norm), j, cos_t, sin_t)
            q = _mla_q(geo, cq, w_uq_p, cos_t, sin_t)
            k, v = _mla_kv(geo, ckv, bf(mla_w_ukv), j, kpe)
            o_parts = [_attn(q, k, v, g1), _attn(q, k, v, g2)]
            x, h = _res(geo, o_parts, bf(mla_w_o), j, None, 0, x_parts, i, mod, G_M, **close)

        u = _gated(h, w_gate, i, 0, w_up, i, 0, d_ff, None, 0, mode="swiglu")
        ffn = (geo, [u], w_dn, i, None, 0, [x], i, mod, G_F)
        if i + 1 < depth:
            x, h = _res(*ffn, emit=Emit(ln_mix3, i + 1, SC_M, SH_M, False), tm_want=256, **full)
            x_parts = [x]

    last = Emit(final_norm.reshape(1, 1, D), 0, 0, 0, True)
    y1, y2 = (_res(*ffn, emit=last, rows=(g.row_off, g.B * g.S), tm_want=256, **full) for g in (g1, g2))
    return (y1.reshape(B1, S1, D), y2.reshape(B2, S2, D))
```

```python
import functools
import math
from typing import NamedTuple

import jax
import jax.numpy as jnp
from jax import lax
from jax.experimental import pallas as pl
from jax.experimental.pallas import tpu as pltpu

F32 = jnp.float32
BF16 = jnp.bfloat16

EPS = 1e-6
N_MIXERS = 3
FNET_GROUPS = 8
CONV_WIDTH = 31
CONV_PAD = CONV_WIDTH // 2
MLA_HEADS = 16
MLA_Q_RANK = 512
MLA_KV_RANK = 512
MLA_NOPE_DIM = 128
MLA_ROPE_DIM = 64
MLA_V_DIM = 128
MLA_QK_DIM = MLA_NOPE_DIM + MLA_ROPE_DIM
MLA_SCALE = 1.0 / math.sqrt(MLA_QK_DIM)
LOG2E = math.log2(math.e)
ROPE_BASE = 10000.0

LANES = 128
SUBLANES = 8
BF16_ROWS = 16
MXU_WIDTH = 256
GATED_ROWS = 1024
CONV_HALO = 2 * SUBLANES
CONV_STRIDE = 4
CONV_ROWS = CONV_STRIDE * SUBLANES
CONV_TILE = 256
ROPE_PAD = LANES
QK_PAD = MLA_NOPE_DIM + ROPE_PAD
ATTN_HEADS = 2
DFT_RADIX = 4
VMEM_LIMIT = 56 * 1024 * 1024

SH_M, SC_M, G_M, SH_F, SC_F, G_F = range(6)


class Geo(NamedTuple):
    B1: int
    S1: int
    B2: int
    S2: int
    D: int

    @property
    def T1(self):
        return self.B1 * self.S1

    @property
    def T2(self):
        return self.B2 * self.S2

    @property
    def T(self):
        return self.T1 + self.T2

    @property
    def NB(self):
        return self.B1 + self.B2


class Group(NamedTuple):
    B: int
    S: int
    row_off: int
    seq_off: int


def _groups(geo):
    return Group(geo.B1, geo.S1, 0, 0), Group(geo.B2, geo.S2, geo.T1, geo.B1)


def _cparams(sem, vmem=VMEM_LIMIT):
    return pltpu.CompilerParams(dimension_semantics=sem, vmem_limit_bytes=vmem)


def _row_tile(geo, want):
    g = math.gcd(geo.S1, geo.S2)
    t = min(want, g)
    while g % t:
        t //= 2
    return t


def _seq_idx(i, tm, geo):
    n1 = geo.T1 // tm
    return jnp.where(i < n1, i // (geo.S1 // tm), geo.B1 + (i - n1) // (geo.S2 // tm))


def _mod_spec(layer, which, seq_fn):
    def index_map(*g):
        return (layer, seq_fn(*g), which, 0, 0)
    return lambda D: pl.BlockSpec((None, None, None, 1, D), index_map)


def _layer_spec(layer, shape, tail_map, resident=False):
    mode = pl.Buffered(1) if resident else None
    return pl.BlockSpec((None,) + tuple(shape), lambda *g: (layer,) + tuple(tail_map(*g)), pipeline_mode=mode)


def _sigmoid(x):
    return 1.0 / (1.0 + jnp.exp(-x))


def _norm_mod(x, lnw, sc, sh):
    ms = jnp.mean(x * x, axis=-1, keepdims=True)
    y = x * lax.rsqrt(ms + EPS) * lnw
    return y * (1.0 + sc) + sh


def _ada_kernel(c_ref, w_ref, b_ref, o_ref):
    c = c_ref[...]
    a = (c * _sigmoid(c)).astype(BF16)
    o_ref[...] = jnp.dot(a, w_ref[...].astype(BF16), preferred_element_type=F32) + b_ref[...]


def _ada(c_pad, w_ada, b_ada):
    L, D, N = w_ada.shape
    R = c_pad.shape[0]
    tn = min(1024, N)
    return pl.pallas_call(
        _ada_kernel,
        grid=(L, N // tn),
        in_specs=[
            pl.BlockSpec((R, D), lambda l, j: (0, 0)),
            pl.BlockSpec((None, D, tn), lambda l, j: (l, 0, j)),
            pl.BlockSpec((None, 1, tn), lambda l, j: (l, 0, j)),
        ],
        out_specs=pl.BlockSpec((None, R, tn), lambda l, j: (l, 0, j)),
        out_shape=jax.ShapeDtypeStruct((L, R, N), F32),
        compiler_params=_cparams(("parallel", "parallel")),
        name="ada_mod",
    )(c_pad, w_ada, b_ada.reshape(L, 1, N))


def _gate(p1, p2, mode):
    if mode == "swiglu":
        return p1 * _sigmoid(p1) * p2
    return p1 * _sigmoid(p2)


def _gated_kernel(h_ref, w1_ref, w2_ref, *rest, mode, bias):
    if bias:
        b1_ref, b2_ref, o_ref = rest
    else:
        (o_ref,) = rest
    tm, tn = o_ref.shape
    rc = min(GATED_ROWS, tm)
    for r in range(tm // rc):
        rs = slice(r * rc, (r + 1) * rc)
        h = h_ref[rs, :]
        for c in range(tn // MXU_WIDTH):
            cs = slice(c * MXU_WIDTH, (c + 1) * MXU_WIDTH)
            p1 = jnp.dot(h, w1_ref[:, cs].astype(BF16), preferred_element_type=F32)
            p2 = jnp.dot(h, w2_ref[:, cs].astype(BF16), preferred_element_type=F32)
            if bias:
                p1 = p1 + b1_ref[:, cs]
                p2 = p2 + b2_ref[:, cs]
            o_ref[rs, cs] = _gate(p1, p2, mode).astype(o_ref.dtype)


def _gated(h, w1, lw1, off1, w2, lw2, off2, n_out, b, lb, *, mode, tm_want=2048, tn_want=512):
    T, D = h.shape
    tm = math.gcd(tm_want, T)
    tn = math.gcd(tn_want, n_out)
    assert off1 % tn == 0 and off2 % tn == 0 and tn % MXU_WIDTH == 0
    o1, o2 = off1 // tn, off2 // tn
    bias = b is not None
    in_specs = [pl.BlockSpec((tm, D), lambda i, j: (i, 0)),
                _layer_spec(lw1, (D, tn), lambda i, j: (0, j + o1)),
                _layer_spec(lw2, (D, tn), lambda i, j: (0, j + o2))]
    args = [h, w1, w2]
    if bias:
        in_specs += [_layer_spec(lb, (1, tn), lambda i, j: (0, j + o1)),
                     _layer_spec(lb, (1, tn), lambda i, j: (0, j + o2))]
        args += [b, b]
    return pl.pallas_call(
        functools.partial(_gated_kernel, mode=mode, bias=bias),
        grid=(T // tm, n_out // tn),
        in_specs=in_specs,
        out_specs=pl.BlockSpec((tm, tn), lambda i, j: (i, j)),
        out_shape=jax.ShapeDtypeStruct((T, n_out), BF16),
        compiler_params=_cparams(("parallel", "parallel")),
        name="gated_" + mode,
    )(*args)


class Emit(NamedTuple):
    lnw: jax.Array
    layer: int
    which_sc: int
    which_sh: int
    final: bool


def _res_kernel(*refs, bias, na, nx, n1, emit):
    a_refs = refs[:na]
    w_ref = refs[na]
    pos = na + 1
    b_ref = None
    if bias:
        b_ref = refs[pos]
        pos += 1
    x_refs = refs[pos:pos + nx]
    g_ref = refs[pos + nx]
    pos += nx + 1
    if emit == "final":
        lnw_ref, o_ref = refs[pos:pos + 2]
    elif emit == "mod":
        lnw_ref, sc_ref, sh_ref, o_ref, h_ref = refs[pos:pos + 5]
    else:
        o_ref = refs[pos]

    def compute(a_ref, x_ref):
        y = jnp.dot(a_ref[...], w_ref[...], preferred_element_type=F32)
        if bias:
            y = y + b_ref[...]
        xn = x_ref[...] + g_ref[...] * y
        if emit == "final":
            o_ref[...] = xn * lax.rsqrt(jnp.mean(xn * xn, axis=-1, keepdims=True) + EPS) * lnw_ref[...]
            return
        o_ref[...] = xn
        if emit == "mod":
            h_ref[...] = _norm_mod(xn, lnw_ref[...], sc_ref[...], sh_ref[...]).astype(BF16)

    if na == 1 and nx == 1:
        compute(a_refs[0], x_refs[0])
    else:
        i = pl.program_id(0)
        pl.when(i < n1)(lambda: compute(a_refs[0], x_refs[0]))
        pl.when(i >= n1)(lambda: compute(a_refs[-1], x_refs[-1]))


def _res(geo, a_parts, w, lw, b, lb, x_parts, layer, mod, which_g, *, emit=None, rows=None,
         tm_want=1024, tn_want=512):
    K, D = w.shape[1], w.shape[2]
    row_off, nrows = rows if rows is not None else (0, geo.T)
    tm = _row_tile(geo, tm_want)
    tn = min(tn_want, D)
    n1 = geo.T1 // tm
    ob = row_off // tm
    bias = b is not None
    assert row_off % tm == 0 and (rows is None or len(a_parts) == len(x_parts) == 1)

    def part_specs(parts, width, col):
        if len(parts) == 1:
            return [pl.BlockSpec((tm, width), lambda i, j: (ob + i, col(j)))]
        return [pl.BlockSpec((tm, width), lambda i, j: (jnp.minimum(i, n1 - 1), col(j))),
                pl.BlockSpec((tm, width), lambda i, j: (jnp.maximum(i - n1, 0), col(j)))]

    seq = lambda i, j: _seq_idx(ob + i, tm, geo)
    in_specs = part_specs(a_parts, K, lambda j: 0) + [_layer_spec(lw, (K, tn), lambda i, j: (0, j), resident=tn == D)]
    args = list(a_parts) + [w]
    if bias:
        in_specs.append(_layer_spec(lb, (1, tn), lambda i, j: (0, j)))
        args.append(b)
    in_specs += part_specs(x_parts, tn, lambda j: j)
    in_specs.append(pl.BlockSpec((None, None, None, 1, tn), lambda i, j: (layer, seq(i, j), which_g, 0, j)))
    alias = {len(args): 0} if (len(x_parts) == 1 and rows is None and not (emit and emit.final)) else {}
    args += list(x_parts) + [mod]
    out_specs = pl.BlockSpec((tm, tn), lambda i, j: (i, j))
    out_shape = jax.ShapeDtypeStruct((nrows, D), F32)
    kind = None
    if emit is not None:
        assert tn == D
        in_specs.append(_layer_spec(emit.layer, (1, D), lambda i, j: (0, 0)))
        args.append(emit.lnw)
        kind = "final"
        if not emit.final:
            kind = "mod"
            in_specs += [_mod_spec(emit.layer, emit.which_sc, seq)(D), _mod_spec(emit.layer, emit.which_sh, seq)(D)]
            args += [mod, mod]
            out_specs = [out_specs, pl.BlockSpec((tm, D), lambda i, j: (i, 0))]
            out_shape = [out_shape, jax.ShapeDtypeStruct((nrows, D), BF16)]
    return pl.pallas_call(
        functools.partial(_res_kernel, bias=bias, na=len(a_parts), nx=len(x_parts), n1=n1, emit=kind),
        grid=(nrows // tm, D // tn),
        in_specs=in_specs,
        out_specs=out_specs,
        out_shape=out_shape,
        input_output_aliases=alias,
        compiler_params=_cparams(("parallel", "parallel")),
        name="res_mm",
    )(*args)


def _fnet_chan_kernel(*refs, gd, normed):
    xq_refs = refs[:DFT_RADIX]
    if normed:
        cs_ref, tc_ref, ts_ref, vre_ref, vim_ref = refs[DFT_RADIX:]
        hq = [r[...] for r in xq_refs]
    else:
        lnw_ref, sc_ref, sh_ref, cs_ref, tc_ref, ts_ref, vre_ref, vim_ref = refs[DFT_RADIX:]
        lnw, sc, sh = lnw_ref[...], sc_ref[...], sh_ref[...]
        hq = [_norm_mod(r[...], lnw, sc, sh).astype(BF16) for r in xq_refs]
    cs = cs_ref[...]
    for g in range(hq[0].shape[1] // gd):
        p = [jnp.dot(h[:, g * gd:(g + 1) * gd], cs, preferred_element_type=F32) for h in hq]
        for hh in range(gd // LANES):
            re = [pq[:, hh * LANES:(hh + 1) * LANES] for pq in p]
            im = [pq[:, gd + hh * LANES:gd + (hh + 1) * LANES] for pq in p]
            a02p, a02m = re[0] + re[2], re[0] - re[2]
            a13p, a13m = re[1] + re[3], re[1] - re[3]
            b02p, b02m = im[0] + im[2], im[0] - im[2]
            b13p, b13m = im[1] + im[3], im[1] - im[3]
            ure = [a02p + a13p, a02m + b13m, a02p - a13p, a02m - b13m]
            uim = [b02p + b13p, b02m - a13m, b02p - b13p, b02m + a13m]
            ls = slice(g * gd + hh * LANES, g * gd + (hh + 1) * LANES)
            vre_ref[0, :, ls] = ure[0].astype(BF16)
            vim_ref[0, :, ls] = uim[0].astype(BF16)
            for r in range(1, DFT_RADIX):
                c, s = tc_ref[r - 1], ts_ref[r - 1]
                vre_ref[r, :, ls] = (ure[r] * c + uim[r] * s).astype(BF16)
                vim_ref[r, :, ls] = (uim[r] * c - ure[r] * s).astype(BF16)


def _fnet_chan(x, row_off, grp, lnw, layer, mod, cs, *, normed, tm_want=256):
    D = x.shape[1]
    gd = cs.shape[0]
    B, S = grp.B, grp.S
    N = S // DFT_RADIX
    tm = min(tm_want, N)
    assert N % tm == 0 and row_off % tm == 0 and gd % LANES == 0
    s = jnp.arange(N, dtype=jnp.int32)[None, :, None]
    r = jnp.arange(1, DFT_RADIX, dtype=jnp.int32)[:, None, None]
    ang = jnp.broadcast_to(((r * s) % S).astype(F32) * (2.0 * math.pi / S), (DFT_RADIX - 1, N, LANES))
    xq = lambda q: pl.BlockSpec((tm, D), lambda b, i: ((row_off + q * N) // tm + b * (S // tm) + i, 0))
    seq = lambda b, i: grp.seq_off + b
    tw = pl.BlockSpec((DFT_RADIX - 1, tm, LANES), lambda b, i: (0, i, 0))
    out = pl.BlockSpec((None, DFT_RADIX, tm, D), lambda b, i: (b, 0, i, 0))
    in_specs = [xq(q) for q in range(DFT_RADIX)]
    args = [x] * DFT_RADIX
    if not normed:
        in_specs += [_layer_spec(layer, (1, D), lambda b, i: (0, 0)),
                     _mod_spec(layer, SC_M, seq)(D), _mod_spec(layer, SH_M, seq)(D)]
        args += [lnw, mod, mod]
    in_specs += [pl.BlockSpec((gd, 2 * gd), lambda b, i: (0, 0)), tw, tw]
    args += [cs, jnp.cos(ang), jnp.sin(ang)]
    return pl.pallas_call(
        functools.partial(_fnet_chan_kernel, gd=gd, normed=normed),
        grid=(B, N // tm),
        in_specs=in_specs,
        out_specs=[out, out],
        out_shape=[jax.ShapeDtypeStruct((B, DFT_RADIX, N, D), BF16)] * 2,
        compiler_params=_cparams(("parallel", "parallel")),
        name="fnet_chan",
    )(*args)


def _fnet_seq_kernel(c_ref, s_ref, vre_ref, vim_ref, o_ref, slab_ref):
    cw, sw = c_ref[...], s_ref[...]
    tmk = cw.shape[0]
    for r in range(DFT_RADIX):
        y = jnp.dot(cw, vre_ref[r], preferred_element_type=F32)
        y = y + jnp.dot(sw, vim_ref[r], preferred_element_type=F32)
        for c in range(y.shape[1] // LANES):
            slab_ref[c, pl.ds(r, tmk, stride=DFT_RADIX), :] = y[:, c * LANES:(c + 1) * LANES]
    for c in range(o_ref.shape[1] // LANES):
        o_ref[:, c * LANES:(c + 1) * LANES] = slab_ref[c].astype(o_ref.dtype)


def _fnet_seq(vre, vim, grp, *, tmk_want=512, tn_want=512):
    B, S = grp.B, grp.S
    N, D = vre.shape[2], vre.shape[3]
    tmk = min(tmk_want, N)
    tn = min(tn_want, D)
    assert N % tmk == 0 and D % tn == 0 and tn % LANES == 0
    k = jnp.arange(N, dtype=jnp.int32)
    ang = ((k[:, None] * k[None, :]) % N).astype(F32) * (2.0 * math.pi / N)
    nrm = 1.0 / math.sqrt(S)
    cn, sn = (jnp.cos(ang) * nrm).astype(BF16), (jnp.sin(ang) * nrm).astype(BF16)
    wspec = pl.BlockSpec((tmk, N), lambda b, j, kt: (kt, 0))
    vspec = pl.BlockSpec((None, DFT_RADIX, N, tn), lambda b, j, kt: (b, 0, 0, j))
    return pl.pallas_call(
        _fnet_seq_kernel,
        grid=(B, D // tn, N // tmk),
        in_specs=[wspec, wspec, vspec, vspec],
        out_specs=pl.BlockSpec((DFT_RADIX * tmk, tn), lambda b, j, kt: (b * (N // tmk) + kt, j)),
        out_shape=jax.ShapeDtypeStruct((B * S, D), BF16),
        scratch_shapes=[pltpu.VMEM((tn // LANES, DFT_RADIX * tmk, LANES), F32)],
        compiler_params=_cparams(("parallel", "parallel", "arbitrary")),
        name="fnet_seq",
    )(cn, sn, vre, vim)


def _conv_kernel(prev_ref, cur_ref, next_ref, w_ref, bdw_ref, lng_ref, lnb_ref, o_ref, buf_ref, cv_ref,
                 *, ts, geo):
    r0 = pl.program_id(0) * ts
    in1 = r0 < geo.T1
    S = jnp.where(in1, geo.S1, geo.S2)
    rel = jnp.where(in1, r0, r0 - geo.T1)
    first = (rel % S) == 0
    last = ((rel + ts) % S) == 0
    H = CONV_HALO
    D = cur_ref.shape[1]
    nslab = D // LANES
    base = H - CONV_PAD
    nload = CONV_WIDTH + CONV_STRIDE - 1

    for c in range(nslab):
        ls = slice(c * LANES, (c + 1) * LANES)
        buf_ref[c, 0:H, :] = jnp.where(first, 0.0, prev_ref[:, ls].astype(F32))
        buf_ref[c, H:H + ts, :] = cur_ref[:, ls].astype(F32)
        buf_ref[c, H + ts:2 * H + ts, :] = jnp.where(last, 0.0, next_ref[:, ls].astype(F32))

    for c in range(nslab):
        ls = slice(c * LANES, (c + 1) * LANES)
        wts = [jnp.broadcast_to(w_ref[k:k + 1, ls], (SUBLANES, LANES)) for k in range(CONV_WIDTH)]
        bias = jnp.broadcast_to(bdw_ref[:, ls], (SUBLANES, LANES))

        def body(n, carry):
            row = n * CONV_ROWS
            accs = [bias] * CONV_STRIDE
            for t in range(nload):
                v = buf_ref[c, pl.ds(row + base + t, SUBLANES, stride=CONV_STRIDE), :]
                for j in range(CONV_STRIDE):
                    k = t - j
                    if 0 <= k < CONV_WIDTH:
                        accs[j] = accs[j] + v * wts[k]
            for j in range(CONV_STRIDE):
                cv_ref[c, pl.ds(row + j, SUBLANES, stride=CONV_STRIDE), :] = accs[j]
            return carry

        lax.fori_loop(0, ts // CONV_ROWS, body, 0, unroll=True)

    cvv = cv_ref[...]
    mu = jnp.sum(jnp.sum(cvv, axis=0), axis=-1, keepdims=True) * (1.0 / D)
    d = cvv - mu[None]
    var = jnp.sum(jnp.sum(d * d, axis=0), axis=-1, keepdims=True) * (1.0 / D)
    inv = lax.rsqrt(var + EPS)
    for c in range(nslab):
        ls = slice(c * LANES, (c + 1) * LANES)
        y = d[c] * inv * lng_ref[:, ls] + lnb_ref[:, ls]
        o_ref[:, ls] = (y * _sigmoid(y)).astype(o_ref.dtype)


def _conv(geo, u, w_dw, lw, b_dw, ln_g, ln_b, *, ts_want=CONV_TILE):
    T, D = u.shape
    ts = _row_tile(geo, ts_want)
    H = CONV_HALO
    assert ts % CONV_ROWS == 0 and ts % H == 0 and D % LANES == 0
    nh = T // H
    vec = lambda: _layer_spec(lw, (1, D), lambda i: (0, 0))
    return pl.pallas_call(
        functools.partial(_conv_kernel, ts=ts, geo=geo),
        grid=(T // ts,),
        in_specs=[pl.BlockSpec((H, D), lambda i: (jnp.maximum(i * (ts // H) - 1, 0), 0)),
                  pl.BlockSpec((ts, D), lambda i: (i, 0)),
                  pl.BlockSpec((H, D), lambda i: (jnp.minimum((i + 1) * (ts // H), nh - 1), 0)),
                  _layer_spec(lw, (CONV_WIDTH, D), lambda i: (0, 0)),
                  vec(), vec(), vec()],
        out_specs=pl.BlockSpec((ts, D), lambda i: (i, 0)),
        out_shape=jax.ShapeDtypeStruct((T, D), BF16),
        scratch_shapes=[pltpu.VMEM((D // LANES, ts + 2 * H, LANES), F32),
                        pltpu.VMEM((D // LANES, ts, LANES), F32)],
        compiler_params=_cparams(("parallel",)),
        name="dwconv_ln_silu",
    )(u, u, u, w_dw, b_dw, ln_g, ln_b)


def _rope(pe, cos, sin):
    return pe * cos + pltpu.roll(pe, ROPE_PAD // 2, 1) * sin


def _mla_down_kernel(h_ref, w_ref, qn_ref, kvn_ref, cos_ref, sin_ref, cq_ref, ckv_ref, kpe_ref):
    h = h_ref[...]
    qr, kr = MLA_Q_RANK, MLA_KV_RANK
    cq = jnp.dot(h, w_ref[:, :qr], preferred_element_type=F32)
    cq_ref[...] = (cq * lax.rsqrt(jnp.mean(cq * cq, axis=-1, keepdims=True) + EPS) * qn_ref[...]).astype(BF16)
    ckv = jnp.dot(h, w_ref[:, qr:qr + kr], preferred_element_type=F32)
    ckv_ref[...] = (ckv * lax.rsqrt(jnp.mean(ckv * ckv, axis=-1, keepdims=True) + EPS) * kvn_ref[...]).astype(BF16)
    pe = jnp.dot(h, w_ref[:, qr + kr:], preferred_element_type=F32)
    kpe_ref[...] = _rope(pe, cos_ref[...], sin_ref[...]).astype(BF16)


def _pos_spec(geo, tm):
    n1 = geo.T1 // tm

    def index_map(i):
        return (jnp.where(i < n1, i % (geo.S1 // tm), (i - n1) % (geo.S2 // tm)), 0)
    return pl.BlockSpec((tm, ROPE_PAD), index_map)


def _mla_down(geo, h, w_down_p, q_norm, kv_norm, lw, cos_t, sin_t, *, tm_want=512):
    T, D = h.shape
    N = w_down_p.shape[1]
    tm = _row_tile(geo, tm_want)
    row = lambda n: pl.BlockSpec((tm, n), lambda i: (i, 0))
    return pl.pallas_call(
        _mla_down_kernel,
        grid=(T // tm,),
        in_specs=[row(D),
                  pl.BlockSpec((D, N), lambda i: (0, 0)),
                  _layer_spec(lw, (1, MLA_Q_RANK), lambda i: (0, 0)),
                  _layer_spec(lw, (1, MLA_KV_RANK), lambda i: (0, 0)),
                  _pos_spec(geo, tm), _pos_spec(geo, tm)],
        out_specs=[row(MLA_Q_RANK), row(MLA_KV_RANK), row(ROPE_PAD)],
        out_shape=[jax.ShapeDtypeStruct((T, MLA_Q_RANK), BF16),
                   jax.ShapeDtypeStruct((T, MLA_KV_RANK), BF16),
                   jax.ShapeDtypeStruct((T, ROPE_PAD), BF16)],
        compiler_params=_cparams(("parallel",)),
        name="mla_down",
    )(h, w_down_p, q_norm, kv_norm, cos_t, sin_t)


def _mla_q_kernel(cq_ref, w_ref, cos_ref, sin_ref, q_ref, *, qscale):
    a = cq_ref[...]
    cos = cos_ref[...] * qscale
    sin = sin_ref[...] * qscale
    for h in range(MLA_HEADS):
        lo = h * QK_PAD
        acc = jnp.dot(a, w_ref[:, lo:lo + QK_PAD], preferred_element_type=F32)
        q_ref[:, lo:lo + MLA_NOPE_DIM] = (acc[:, :MLA_NOPE_DIM] * qscale).astype(BF16)
        q_ref[:, lo + MLA_NOPE_DIM:lo + QK_PAD] = _rope(acc[:, MLA_NOPE_DIM:], cos, sin).astype(BF16)


def _mla_q(geo, cq, w_uq_p, cos_t, sin_t, *, tm_want=512):
    T, R = cq.shape
    N = w_uq_p.shape[1]
    tm = _row_tile(geo, tm_want)
    return pl.pallas_call(
        functools.partial(_mla_q_kernel, qscale=MLA_SCALE * LOG2E),
        grid=(T // tm,),
        in_specs=[pl.BlockSpec((tm, R), lambda i: (i, 0)),
                  pl.BlockSpec((R, N), lambda i: (0, 0)),
                  _pos_spec(geo, tm), _pos_spec(geo, tm)],
        out_specs=pl.BlockSpec((tm, N), lambda i: (i, 0)),
        out_shape=jax.ShapeDtypeStruct((T, N), BF16),
        compiler_params=_cparams(("parallel",)),
        name="mla_q_up",
    )(cq, w_uq_p, cos_t, sin_t)


def _mla_kv_kernel(ckv_ref, w_ref, kpe_ref, k_ref, v_ref):
    a = ckv_ref[...]
    kpe = kpe_ref[...]
    hw = MLA_NOPE_DIM + MLA_V_DIM
    for h in range(MLA_HEADS):
        acc = jnp.dot(a, w_ref[:, h * hw:(h + 1) * hw], preferred_element_type=F32)
        lo = h * QK_PAD
        k_ref[:, lo:lo + MLA_NOPE_DIM] = acc[:, :MLA_NOPE_DIM].astype(BF16)
        k_ref[:, lo + MLA_NOPE_DIM:lo + QK_PAD] = kpe
        v_ref[:, h * MLA_V_DIM:(h + 1) * MLA_V_DIM] = acc[:, MLA_NOPE_DIM:].astype(BF16)


def _mla_kv(geo, ckv, w_ukv, lw, kpe, *, tm_want=512):
    T, R = ckv.shape
    N = w_ukv.shape[2]
    tm = _row_tile(geo, tm_want)
    nk = MLA_HEADS * QK_PAD
    nv = MLA_HEADS * MLA_V_DIM
    return pl.pallas_call(
        _mla_kv_kernel,
        grid=(T // tm,),
        in_specs=[pl.BlockSpec((tm, R), lambda i: (i, 0)),
                  _layer_spec(lw, (R, N), lambda i: (0, 0)),
                  pl.BlockSpec((tm, ROPE_PAD), lambda i: (i, 0))],
        out_specs=[pl.BlockSpec((tm, nk), lambda i: (i, 0)),
                   pl.BlockSpec((tm, nv), lambda i: (i, 0))],
        out_shape=[jax.ShapeDtypeStruct((T, nk), BF16),
                   jax.ShapeDtypeStruct((T, nv), BF16)],
        compiler_params=_cparams(("parallel",)),
        name="mla_kv_up",
    )(ckv, w_ukv, kpe)


def _attn_kernel(q_ref, k_ref, v_ref, o_ref, *, tk):
    tq = q_ref.shape[0]
    for h in range(ATTN_HEADS):
        qc = slice(h * QK_PAD, (h + 1) * QK_PAD)
        vc = slice(h * MLA_V_DIM, (h + 1) * MLA_V_DIM)
        q = q_ref[:, qc]
        m = jnp.full((tq, 1), -jnp.inf, F32)
        l = jnp.zeros((tq, 1), F32)
        acc = jnp.zeros((tq, MLA_V_DIM), F32)
        for j in range(k_ref.shape[0] // tk):
            ks = slice(j * tk, (j + 1) * tk)
            s = lax.dot_general(q, k_ref[ks, qc], (((1,), (1,)), ((), ())), preferred_element_type=F32)
            m_new = jnp.maximum(m, jnp.max(s, axis=-1, keepdims=True))
            alpha = jnp.exp2(m - m_new)
            p = jnp.exp2(s - m_new)
            l = alpha * l + jnp.sum(p, axis=-1, keepdims=True)
            acc = alpha * acc + jnp.dot(p.astype(BF16), v_ref[ks, vc], preferred_element_type=F32)
            m = m_new
        o_ref[:, vc] = (acc / l).astype(o_ref.dtype)


def _attn(q, k, v, grp, *, tq_want=1024, tk_want=1024):
    B, S = grp.B, grp.S
    tq = min(tq_want, S)
    tk = min(tk_want, S)
    assert grp.row_off % S == 0 and MLA_HEADS % ATTN_HEADS == 0
    ob = grp.row_off // S
    oq = grp.row_off // tq
    nq = S // tq
    nh = ATTN_HEADS
    return pl.pallas_call(
        functools.partial(_attn_kernel, tk=tk),
        grid=(B, MLA_HEADS // nh, nq),
        in_specs=[pl.BlockSpec((tq, nh * QK_PAD), lambda b, h, i: (oq + b * nq + i, h)),
                  pl.BlockSpec((S, nh * QK_PAD), lambda b, h, i: (ob + b, h)),
                  pl.BlockSpec((S, nh * MLA_V_DIM), lambda b, h, i: (ob + b, h))],
        out_specs=pl.BlockSpec((tq, nh * MLA_V_DIM), lambda b, h, i: (b * nq + i, h)),
        out_shape=jax.ShapeDtypeStruct((B * S, MLA_HEADS * MLA_V_DIM), BF16),
        compiler_params=_cparams(("parallel", "parallel", "arbitrary")),
        name="mla_attn",
    )(q, k, v)


def _chan_dft_matrix(gd):
    k = jnp.arange(gd, dtype=jnp.int32)
    ang = ((k[:, None] * k[None, :]) % gd).astype(F32) * (2.0 * math.pi / gd)
    nrm = 1.0 / math.sqrt(gd)
    return jnp.concatenate([jnp.cos(ang) * nrm, -jnp.sin(ang) * nrm], axis=1).astype(BF16)


def _rope_tables(geo):
    half = MLA_ROPE_DIM // 2
    inv_freq = 1.0 / (ROPE_BASE ** (jnp.arange(0, MLA_ROPE_DIM, 2, dtype=F32) / MLA_ROPE_DIM))
    pos = jnp.arange(max(geo.S1, geo.S2), dtype=F32)
    ang = pos[:, None] * inv_freq[None, :]
    c, s = jnp.cos(ang), jnp.sin(ang)
    z = jnp.zeros_like(c)
    assert ROPE_PAD == 4 * half
    return jnp.concatenate([c, z, c, z], axis=1), jnp.concatenate([-s, z, s, z], axis=1)


def _spread_rope_cols(w):
    half = MLA_ROPE_DIM // 2
    z = jnp.zeros(w.shape[:-1] + (half,), w.dtype)
    return jnp.concatenate([w[..., :half], z, w[..., half:], z], axis=-1)


def kernel(x_prompt, x_sample, c_prompt, c_sample, ln_mix, ln_ffn, w_ada, b_ada, fnet_w_o, fnet_b_o, conv_w_in, conv_b_in, conv_w_dw, conv_b_dw, conv_ln_g, conv_ln_b, conv_w_out, conv_b_out, mla_w_down, mla_q_norm, mla_w_uq, mla_kv_norm, mla_w_ukv, mla_w_o, ffn_w_gate, ffn_w_up, ffn_w_down, final_norm):
    B1, S1, D = x_prompt.shape
    B2, S2, _ = x_sample.shape
    geo = Geo(B1, S1, B2, S2, D)
    g1, g2 = _groups(geo)
    depth = ln_mix.shape[0]
    NB = geo.NB

    x_parts = [x_prompt.reshape(geo.T1, D), x_sample.reshape(geo.T2, D)]

    c_all = jnp.concatenate([c_prompt, c_sample], axis=0)
    c_pad = jnp.pad(c_all, ((0, -NB % BF16_ROWS), (0, 0)))
    mod = _ada(c_pad, w_ada, b_ada)[:, :NB].reshape(depth, NB, 6, 1, D)

    bf = lambda w: w.astype(BF16)
    vec = lambda v: v.reshape(v.shape[0], 1, v.shape[1])
    ln_mix3, ln_ffn3 = vec(ln_mix), vec(ln_ffn)
    w_gate, w_up, w_dn = ffn_w_gate, ffn_w_up, bf(ffn_w_down)
    d_ff = ffn_w_gate.shape[2]
    full = dict(tn_want=D)
    h = None

    for i in range(depth):
        kind, j = i % N_MIXERS, i // N_MIXERS
        close = dict(emit=Emit(ln_ffn3, i, SC_F, SH_F, False), tm_want=512, **full)
        if kind == 0:
            cs = _chan_dft_matrix(D // FNET_GROUPS)
            f_parts = []
            for n, grp in enumerate((g1, g2)):
                if h is None:
                    vre, vim = _fnet_chan(x_parts[n], 0, grp, ln_mix3, i, mod, cs, normed=False)
                else:
                    vre, vim = _fnet_chan(h, grp.row_off, grp, None, i, None, cs, normed=True)
                f_parts.append(_fnet_seq(vre, vim, grp))
            x, h = _res(geo, f_parts, bf(fnet_w_o), j, vec(fnet_b_o), j, x_parts, i, mod, G_M, **close)
        elif kind == 1:
            u = _gated(h, conv_w_in, j, 0, conv_w_in, j, D, D, vec(conv_b_in), j, mode="glu")
            u = _conv(geo, u, conv_w_dw, j, vec(conv_b_dw), vec(conv_ln_g), vec(conv_ln_b))
            x, h = _res(geo, [u], bf(conv_w_out), j, vec(conv_b_out), j, x_parts, i, mod, G_M, **close)
        else:
            cos_t, sin_t = _rope_tables(geo)
            qk = MLA_Q_RANK + MLA_KV_RANK
            w_down_p = bf(jnp.concatenate([mla_w_down[j][:, :qk], _spread_rope_cols(mla_w_down[j][:, qk:])], axis=1))
            w_uq = mla_w_uq[j].reshape(MLA_Q_RANK, MLA_HEADS, MLA_QK_DIM)
            w_uq_p = bf(jnp.concatenate([w_uq[..., :MLA_NOPE_DIM], _spread_rope_cols(w_uq[..., MLA_NOPE_DIM:])],
                                        axis=-1).reshape(MLA_Q_RANK, MLA_HEADS * QK_PAD))
            cq, ckv, kpe = _mla_down(geo, h, w_down_p, vec(mla_q_norm), vec(mla_kv_norm), j, cos_t, sin_t)
            q = _mla_q(geo, cq, w_uq_p, cos_t, sin_t)
            k, v = _mla_kv(geo, ckv, bf(mla_w_ukv), j, kpe)
            o_parts = [_attn(q, k, v, g1), _attn(q, k, v, g2)]
            x, h = _res(geo, o_parts, bf(mla_w_o), j, None, 0, x_parts, i, mod, G_M, **close)

        u = _gated(h, w_gate, i, 0, w_up, i, 0, d_ff, None, 0, mode="swiglu")
        ffn = (geo, [u], w_dn, i, None, 0, [x], i, mod, G_F)
        if i + 1 < depth:
            x, h = _res(*ffn, emit=Emit(ln_mix3, i + 1, SC_M, SH_M, False), tm_want=256, **full)
            x_parts = [x]

    last = Emit(final_norm.reshape(1, 1, D), 0, 0, 0, True)
    y1, y2 = (_res(*ffn, emit=last, rows=(g.row_off, g.B * g.S), tm_want=256, **full) for g in (g1, g2))
    return (y1.reshape(B1, S1, D), y2.reshape(B2, S2, D))
```

```python
import functools
import math
from typing import NamedTuple

import jax
import jax.numpy as jnp
from jax import lax
from jax.experimental import pallas as pl
from jax.experimental.pallas import tpu as pltpu

F32 = jnp.float32
BF16 = jnp.bfloat16

EPS = 1e-6
N_MIXERS = 3
FNET_GROUPS = 8
CONV_WIDTH = 31
CONV_PAD = CONV_WIDTH // 2
MLA_HEADS = 16
MLA_Q_RANK = 512
MLA_KV_RANK = 512
MLA_NOPE_DIM = 128
MLA_ROPE_DIM = 64
MLA_V_DIM = 128
MLA_QK_DIM = MLA_NOPE_DIM + MLA_ROPE_DIM
MLA_SCALE = 1.0 / math.sqrt(MLA_QK_DIM)
LOG2E = math.log2(math.e)
ROPE_BASE = 10000.0

LANES = 128
SUBLANES = 8
BF16_ROWS = 16
MXU_WIDTH = 256
GATED_ROWS = 1024
CONV_HALO = 2 * SUBLANES
CONV_STRIDE = 4
CONV_ROWS = CONV_STRIDE * SUBLANES
CONV_TILE = 256
ROPE_PAD = LANES
QK_PAD = MLA_NOPE_DIM + ROPE_PAD
ATTN_HEADS = 2
MLA_DOWN_ROWS = 256
DFT_RADIX = 4
VMEM_LIMIT = 56 * 1024 * 1024

SH_M, SC_M, G_M, SH_F, SC_F, G_F = range(6)


class Geo(NamedTuple):
    B1: int
    S1: int
    B2: int
    S2: int
    D: int

    @property
    def T1(self):
        return self.B1 * self.S1

    @property
    def T2(self):
        return self.B2 * self.S2

    @property
    def T(self):
        return self.T1 + self.T2

    @property
    def NB(self):
        return self.B1 + self.B2


class Group(NamedTuple):
    B: int
    S: int
    row_off: int
    seq_off: int


def _groups(geo):
    return Group(geo.B1, geo.S1, 0, 0), Group(geo.B2, geo.S2, geo.T1, geo.B1)


def _cparams(sem, vmem=VMEM_LIMIT):
    return pltpu.CompilerParams(dimension_semantics=sem, vmem_limit_bytes=vmem)


def _row_tile(geo, want):
    g = math.gcd(geo.S1, geo.S2)
    t = min(want, g)
    while g % t:
        t //= 2
    return t


def _seq_idx(i, tm, geo):
    n1 = geo.T1 // tm
    return jnp.where(i < n1, i // (geo.S1 // tm), geo.B1 + (i - n1) // (geo.S2 // tm))


def _mod_spec(layer, which, seq_fn):
    def index_map(*g):
        return (layer, seq_fn(*g), which, 0, 0)
    return lambda D: pl.BlockSpec((None, None, None, 1, D), index_map)


def _layer_spec(layer, shape, tail_map, resident=False):
    mode = pl.Buffered(1) if resident else None
    return pl.BlockSpec((None,) + tuple(shape), lambda *g: (layer,) + tuple(tail_map(*g)), pipeline_mode=mode)


def _sigmoid(x):
    return 1.0 / (1.0 + jnp.exp(-x))


def _norm_mod(x, lnw, sc, sh):
    ms = jnp.mean(x * x, axis=-1, keepdims=True)
    y = x * lax.rsqrt(ms + EPS) * lnw
    return y * (1.0 + sc) + sh


def _ada_kernel(c_ref, w_ref, b_ref, o_ref):
    c = c_ref[...]
    a = (c * _sigmoid(c)).astype(BF16)
    o_ref[...] = jnp.dot(a, w_ref[...].astype(BF16), preferred_element_type=F32) + b_ref[...]


def _ada(c_pad, w_ada, b_ada):
    L, D, N = w_ada.shape
    R = c_pad.shape[0]
    tn = min(1024, N)
    return pl.pallas_call(
        _ada_kernel,
        grid=(L, N // tn),
        in_specs=[
            pl.BlockSpec((R, D), lambda l, j: (0, 0)),
            pl.BlockSpec((None, D, tn), lambda l, j: (l, 0, j)),
            pl.BlockSpec((None, 1, tn), lambda l, j: (l, 0, j)),
        ],
        out_specs=pl.BlockSpec((None, R, tn), lambda l, j: (l, 0, j)),
        out_shape=jax.ShapeDtypeStruct((L, R, N), F32),
        compiler_params=_cparams(("parallel", "parallel")),
        name="ada_mod",
    )(c_pad, w_ada, b_ada.reshape(L, 1, N))


def _gate(p1, p2, mode):
    if mode == "swiglu":
        return p1 * _sigmoid(p1) * p2
    return p1 * _sigmoid(p2)


def _gated_kernel(h_ref, w1_ref, w2_ref, *rest, mode, bias):
    if bias:
        b1_ref, b2_ref, o_ref = rest
    else:
        (o_ref,) = rest
    tm, tn = o_ref.shape
    rc = min(GATED_ROWS, tm)
    for r in range(tm // rc):
        rs = slice(r * rc, (r + 1) * rc)
        h = h_ref[rs, :]
        for c in range(tn // MXU_WIDTH):
            cs = slice(c * MXU_WIDTH, (c + 1) * MXU_WIDTH)
            p1 = jnp.dot(h, w1_ref[:, cs].astype(BF16), preferred_element_type=F32)
            p2 = jnp.dot(h, w2_ref[:, cs].astype(BF16), preferred_element_type=F32)
            if bias:
                p1 = p1 + b1_ref[:, cs]
                p2 = p2 + b2_ref[:, cs]
            o_ref[rs, cs] = _gate(p1, p2, mode).astype(o_ref.dtype)


def _gated(h, w1, lw1, off1, w2, lw2, off2, n_out, b, lb, *, mode, tm_want=2048, tn_want=512):
    T, D = h.shape
    tm = math.gcd(tm_want, T)
    tn = math.gcd(tn_want, n_out)
    assert off1 % tn == 0 and off2 % tn == 0 and tn % MXU_WIDTH == 0
    o1, o2 = off1 // tn, off2 // tn
    bias = b is not None
    in_specs = [pl.BlockSpec((tm, D), lambda i, j: (i, 0)),
                _layer_spec(lw1, (D, tn), lambda i, j: (0, j + o1)),
                _layer_spec(lw2, (D, tn), lambda i, j: (0, j + o2))]
    args = [h, w1, w2]
    if bias:
        in_specs += [_layer_spec(lb, (1, tn), lambda i, j: (0, j + o1)),
                     _layer_spec(lb, (1, tn), lambda i, j: (0, j + o2))]
        args += [b, b]
    return pl.pallas_call(
        functools.partial(_gated_kernel, mode=mode, bias=bias),
        grid=(T // tm, n_out // tn),
        in_specs=in_specs,
        out_specs=pl.BlockSpec((tm, tn), lambda i, j: (i, j)),
        out_shape=jax.ShapeDtypeStruct((T, n_out), BF16),
        compiler_params=_cparams(("parallel", "parallel")),
        name="gated_" + mode,
    )(*args)


class Emit(NamedTuple):
    lnw: jax.Array
    layer: int
    which_sc: int
    which_sh: int
    final: bool


def _res_kernel(*refs, bias, na, nx, n1, emit):
    a_refs = refs[:na]
    w_ref = refs[na]
    pos = na + 1
    b_ref = None
    if bias:
        b_ref = refs[pos]
        pos += 1
    x_refs = refs[pos:pos + nx]
    g_ref = refs[pos + nx]
    pos += nx + 1
    if emit == "final":
        lnw_ref, o_ref = refs[pos:pos + 2]
    elif emit == "mod":
        lnw_ref, sc_ref, sh_ref, o_ref, h_ref = refs[pos:pos + 5]
    else:
        o_ref = refs[pos]

    def compute(a_ref, x_ref):
        y = jnp.dot(a_ref[...], w_ref[...], preferred_element_type=F32)
        if bias:
            y = y + b_ref[...]
        xn = x_ref[...] + g_ref[...] * y
        if emit == "final":
            o_ref[...] = xn * lax.rsqrt(jnp.mean(xn * xn, axis=-1, keepdims=True) + EPS) * lnw_ref[...]
            return
        o_ref[...] = xn
        if emit == "mod":
            h_ref[...] = _norm_mod(xn, lnw_ref[...], sc_ref[...], sh_ref[...]).astype(BF16)

    if na == 1 and nx == 1:
        compute(a_refs[0], x_refs[0])
    else:
        i = pl.program_id(0)
        pl.when(i < n1)(lambda: compute(a_refs[0], x_refs[0]))
        pl.when(i >= n1)(lambda: compute(a_refs[-1], x_refs[-1]))


def _res(geo, a_parts, w, lw, b, lb, x_parts, layer, mod, which_g, *, emit=None, rows=None,
         tm_want=1024, tn_want=512):
    K, D = w.shape[1], w.shape[2]
    row_off, nrows = rows if rows is not None else (0, geo.T)
    tm = _row_tile(geo, tm_want)
    tn = min(tn_want, D)
    n1 = geo.T1 // tm
    ob = row_off // tm
    bias = b is not None
    assert row_off % tm == 0 and (rows is None or len(a_parts) == len(x_parts) == 1)

    def part_specs(parts, width, col):
        if len(parts) == 1:
            return [pl.BlockSpec((tm, width), lambda i, j: (ob + i, col(j)))]
        return [pl.BlockSpec((tm, width), lambda i, j: (jnp.minimum(i, n1 - 1), col(j))),
                pl.BlockSpec((tm, width), lambda i, j: (jnp.maximum(i - n1, 0), col(j)))]

    seq = lambda i, j: _seq_idx(ob + i, tm, geo)
    in_specs = part_specs(a_parts, K, lambda j: 0) + [_layer_spec(lw, (K, tn), lambda i, j: (0, j), resident=tn == D)]
    args = list(a_parts) + [w]
    if bias:
        in_specs.append(_layer_spec(lb, (1, tn), lambda i, j: (0, j)))
        args.append(b)
    in_specs += part_specs(x_parts, tn, lambda j: j)
    in_specs.append(pl.BlockSpec((None, None, None, 1, tn), lambda i, j: (layer, seq(i, j), which_g, 0, j)))
    alias = {len(args): 0} if (len(x_parts) == 1 and rows is None and not (emit and emit.final)) else {}
    args += list(x_parts) + [mod]
    out_specs = pl.BlockSpec((tm, tn), lambda i, j: (i, j))
    out_shape = jax.ShapeDtypeStruct((nrows, D), F32)
    kind = None
    if emit is not None:
        assert tn == D
        in_specs.append(_layer_spec(emit.layer, (1, D), lambda i, j: (0, 0)))
        args.append(emit.lnw)
        kind = "final"
        if not emit.final:
            kind = "mod"
            in_specs += [_mod_spec(emit.layer, emit.which_sc, seq)(D), _mod_spec(emit.layer, emit.which_sh, seq)(D)]
            args += [mod, mod]
            out_specs = [out_specs, pl.BlockSpec((tm, D), lambda i, j: (i, 0))]
            out_shape = [out_shape, jax.ShapeDtypeStruct((nrows, D), BF16)]
    return pl.pallas_call(
        functools.partial(_res_kernel, bias=bias, na=len(a_parts), nx=len(x_parts), n1=n1, emit=kind),
        grid=(nrows // tm, D // tn),
        in_specs=in_specs,
        out_specs=out_specs,
        out_shape=out_shape,
        input_output_aliases=alias,
        compiler_params=_cparams(("parallel", "parallel")),
        name="res_mm",
    )(*args)


def _fnet_chan_kernel(*refs, gd, normed):
    xq_refs = refs[:DFT_RADIX]
    if normed:
        cs_ref, tc_ref, ts_ref, vre_ref, vim_ref = refs[DFT_RADIX:]
        hq = [r[...] for r in xq_refs]
    else:
        lnw_ref, sc_ref, sh_ref, cs_ref, tc_ref, ts_ref, vre_ref, vim_ref = refs[DFT_RADIX:]
        lnw, sc, sh = lnw_ref[...], sc_ref[...], sh_ref[...]
        hq = [_norm_mod(r[...], lnw, sc, sh).astype(BF16) for r in xq_refs]
    cs = cs_ref[...]
    for g in range(hq[0].shape[1] // gd):
        p = [jnp.dot(h[:, g * gd:(g + 1) * gd], cs, preferred_element_type=F32) for h in hq]
        for hh in range(gd // LANES):
            re = [pq[:, hh * LANES:(hh + 1) * LANES] for pq in p]
            im = [pq[:, gd + hh * LANES:gd + (hh + 1) * LANES] for pq in p]
            a02p, a02m = re[0] + re[2], re[0] - re[2]
            a13p, a13m = re[1] + re[3], re[1] - re[3]
            b02p, b02m = im[0] + im[2], im[0] - im[2]
            b13p, b13m = im[1] + im[3], im[1] - im[3]
            ure = [a02p + a13p, a02m + b13m, a02p - a13p, a02m - b13m]
            uim = [b02p + b13p, b02m - a13m, b02p - b13p, b02m + a13m]
            ls = slice(g * gd + hh * LANES, g * gd + (hh + 1) * LANES)
            vre_ref[0, :, ls] = ure[0].astype(BF16)
            vim_ref[0, :, ls] = uim[0].astype(BF16)
            for r in range(1, DFT_RADIX):
                c, s = tc_ref[r - 1], ts_ref[r - 1]
                vre_ref[r, :, ls] = (ure[r] * c + uim[r] * s).astype(BF16)
                vim_ref[r, :, ls] = (uim[r] * c - ure[r] * s).astype(BF16)


def _fnet_chan(x, row_off, grp, lnw, layer, mod, cs, *, normed, tm_want=256):
    D = x.shape[1]
    gd = cs.shape[0]
    B, S = grp.B, grp.S
    N = S // DFT_RADIX
    tm = min(tm_want, N)
    assert N % tm == 0 and row_off % tm == 0 and gd % LANES == 0
    s = jnp.arange(N, dtype=jnp.int32)[None, :, None]
    r = jnp.arange(1, DFT_RADIX, dtype=jnp.int32)[:, None, None]
    ang = jnp.broadcast_to(((r * s) % S).astype(F32) * (2.0 * math.pi / S), (DFT_RADIX - 1, N, LANES))
    xq = lambda q: pl.BlockSpec((tm, D), lambda b, i: ((row_off + q * N) // tm + b * (S // tm) + i, 0))
    seq = lambda b, i: grp.seq_off + b
    tw = pl.BlockSpec((DFT_RADIX - 1, tm, LANES), lambda b, i: (0, i, 0))
    out = pl.BlockSpec((None, DFT_RADIX, tm, D), lambda b, i: (b, 0, i, 0))
    in_specs = [xq(q) for q in range(DFT_RADIX)]
    args = [x] * DFT_RADIX
    if not normed:
        in_specs += [_layer_spec(layer, (1, D), lambda b, i: (0, 0)),
                     _mod_spec(layer, SC_M, seq)(D), _mod_spec(layer, SH_M, seq)(D)]
        args += [lnw, mod, mod]
    in_specs += [pl.BlockSpec((gd, 2 * gd), lambda b, i: (0, 0)), tw, tw]
    args += [cs, jnp.cos(ang), jnp.sin(ang)]
    return pl.pallas_call(
        functools.partial(_fnet_chan_kernel, gd=gd, normed=normed),
        grid=(B, N // tm),
        in_specs=in_specs,
        out_specs=[out, out],
        out_shape=[jax.ShapeDtypeStruct((B, DFT_RADIX, N, D), BF16)] * 2,
        compiler_params=_cparams(("parallel", "parallel")),
        name="fnet_chan",
    )(*args)


def _fnet_seq_kernel(c_ref, s_ref, vre_ref, vim_ref, o_ref, slab_ref):
    cw, sw = c_ref[...], s_ref[...]
    tmk = cw.shape[0]
    for r in range(DFT_RADIX):
        y = jnp.dot(cw, vre_ref[r], preferred_element_type=F32)
        y = y + jnp.dot(sw, vim_ref[r], preferred_element_type=F32)
        for c in range(y.shape[1] // LANES):
            slab_ref[c, pl.ds(r, tmk, stride=DFT_RADIX), :] = y[:, c * LANES:(c + 1) * LANES]
    for c in range(o_ref.shape[1] // LANES):
        o_ref[:, c * LANES:(c + 1) * LANES] = slab_ref[c].astype(o_ref.dtype)


def _fnet_seq(vre, vim, grp, *, tmk_want=512, tn_want=512):
    B, S = grp.B, grp.S
    N, D = vre.shape[2], vre.shape[3]
    tmk = min(tmk_want, N)
    tn = min(tn_want, D)
    assert N % tmk == 0 and D % tn == 0 and tn % LANES == 0
    k = jnp.arange(N, dtype=jnp.int32)
    ang = ((k[:, None] * k[None, :]) % N).astype(F32) * (2.0 * math.pi / N)
    nrm = 1.0 / math.sqrt(S)
    cn, sn = (jnp.cos(ang) * nrm).astype(BF16), (jnp.sin(ang) * nrm).astype(BF16)
    wspec = pl.BlockSpec((tmk, N), lambda b, j, kt: (kt, 0))
    vspec = pl.BlockSpec((None, DFT_RADIX, N, tn), lambda b, j, kt: (b, 0, 0, j))
    return pl.pallas_call(
        _fnet_seq_kernel,
        grid=(B, D // tn, N // tmk),
        in_specs=[wspec, wspec, vspec, vspec],
        out_specs=pl.BlockSpec((DFT_RADIX * tmk, tn), lambda b, j, kt: (b * (N // tmk) + kt, j)),
        out_shape=jax.ShapeDtypeStruct((B * S, D), BF16),
        scratch_shapes=[pltpu.VMEM((tn // LANES, DFT_RADIX * tmk, LANES), F32)],
        compiler_params=_cparams(("parallel", "parallel", "arbitrary")),
        name="fnet_seq",
    )(cn, sn, vre, vim)


def _conv_kernel(prev_ref, cur_ref, next_ref, w_ref, bdw_ref, lng_ref, lnb_ref, o_ref, buf_ref, cv_ref,
                 *, ts, geo):
    r0 = pl.program_id(0) * ts
    in1 = r0 < geo.T1
    S = jnp.where(in1, geo.S1, geo.S2)
    rel = jnp.where(in1, r0, r0 - geo.T1)
    first = (rel % S) == 0
    last = ((rel + ts) % S) == 0
    H = CONV_HALO
    D = cur_ref.shape[1]
    nslab = D // LANES
    base = H - CONV_PAD
    nload = CONV_WIDTH + CONV_STRIDE - 1

    for c in range(nslab):
        ls = slice(c * LANES, (c + 1) * LANES)
        buf_ref[c, 0:H, :] = jnp.where(first, 0.0, prev_ref[:, ls].astype(F32))
        buf_ref[c, H:H + ts, :] = cur_ref[:, ls].astype(F32)
        buf_ref[c, H + ts:2 * H + ts, :] = jnp.where(last, 0.0, next_ref[:, ls].astype(F32))

    for c in range(nslab):
        ls = slice(c * LANES, (c + 1) * LANES)
        wts = [jnp.broadcast_to(w_ref[k:k + 1, ls], (SUBLANES, LANES)) for k in range(CONV_WIDTH)]
        bias = jnp.broadcast_to(bdw_ref[:, ls], (SUBLANES, LANES))

        def body(n, carry):
            row = n * CONV_ROWS
            accs = [bias] * CONV_STRIDE
            for t in range(nload):
                v = buf_ref[c, pl.ds(row + base + t, SUBLANES, stride=CONV_STRIDE), :]
                for j in range(CONV_STRIDE):
                    k = t - j
                    if 0 <= k < CONV_WIDTH:
                        accs[j] = accs[j] + v * wts[k]
            for j in range(CONV_STRIDE):
                cv_ref[c, pl.ds(row + j, SUBLANES, stride=CONV_STRIDE), :] = accs[j]
            return carry

        lax.fori_loop(0, ts // CONV_ROWS, body, 0, unroll=True)

    cvv = cv_ref[...]
    mu = jnp.sum(jnp.sum(cvv, axis=0), axis=-1, keepdims=True) * (1.0 / D)
    d = cvv - mu[None]
    var = jnp.sum(jnp.sum(d * d, axis=0), axis=-1, keepdims=True) * (1.0 / D)
    inv = lax.rsqrt(var + EPS)
    for c in range(nslab):
        ls = slice(c * LANES, (c + 1) * LANES)
        y = d[c] * inv * lng_ref[:, ls] + lnb_ref[:, ls]
        o_ref[:, ls] = (y * _sigmoid(y)).astype(o_ref.dtype)


def _conv(geo, u, w_dw, lw, b_dw, ln_g, ln_b, *, ts_want=CONV_TILE):
    T, D = u.shape
    ts = _row_tile(geo, ts_want)
    H = CONV_HALO
    assert ts % CONV_ROWS == 0 and ts % H == 0 and D % LANES == 0
    nh = T // H
    vec = lambda: _layer_spec(lw, (1, D), lambda i: (0, 0))
    return pl.pallas_call(
        functools.partial(_conv_kernel, ts=ts, geo=geo),
        grid=(T // ts,),
        in_specs=[pl.BlockSpec((H, D), lambda i: (jnp.maximum(i * (ts // H) - 1, 0), 0)),
                  pl.BlockSpec((ts, D), lambda i: (i, 0)),
                  pl.BlockSpec((H, D), lambda i: (jnp.minimum((i + 1) * (ts // H), nh - 1), 0)),
                  _layer_spec(lw, (CONV_WIDTH, D), lambda i: (0, 0)),
                  vec(), vec(), vec()],
        out_specs=pl.BlockSpec((ts, D), lambda i: (i, 0)),
        out_shape=jax.ShapeDtypeStruct((T, D), BF16),
        scratch_shapes=[pltpu.VMEM((D // LANES, ts + 2 * H, LANES), F32),
                        pltpu.VMEM((D // LANES, ts, LANES), F32)],
        compiler_params=_cparams(("parallel",)),
        name="dwconv_ln_silu",
    )(u, u, u, w_dw, b_dw, ln_g, ln_b)


def _rope(pe, cos, sin):
    return pe * cos + pltpu.roll(pe, ROPE_PAD // 2, 1) * sin


def _mla_down_kernel(h_ref, w_ref, qn_ref, kvn_ref, cos_ref, sin_ref, cq_ref, ckv_ref, kpe_ref):
    qr, kr = MLA_Q_RANK, MLA_KV_RANK
    tm = h_ref.shape[0]
    rc = min(MLA_DOWN_ROWS, tm)
    for r in range(tm // rc):
        rs = slice(r * rc, (r + 1) * rc)
        h = h_ref[rs, :]
        cq = jnp.dot(h, w_ref[:, :qr], preferred_element_type=F32)
        cq_ref[rs, :] = (cq * lax.rsqrt(jnp.mean(cq * cq, axis=-1, keepdims=True) + EPS) * qn_ref[...]).astype(BF16)
        ckv = jnp.dot(h, w_ref[:, qr:qr + kr], preferred_element_type=F32)
        ckv_ref[rs, :] = (ckv * lax.rsqrt(jnp.mean(ckv * ckv, axis=-1, keepdims=True) + EPS)
                          * kvn_ref[...]).astype(BF16)
        pe = jnp.dot(h, w_ref[:, qr + kr:], preferred_element_type=F32)
        kpe_ref[rs, :] = _rope(pe, cos_ref[rs, :], sin_ref[rs, :]).astype(BF16)


def _pos_spec(geo, tm):
    n1 = geo.T1 // tm

    def index_map(i):
        return (jnp.where(i < n1, i % (geo.S1 // tm), (i - n1) % (geo.S2 // tm)), 0)
    return pl.BlockSpec((tm, ROPE_PAD), index_map)


def _mla_down(geo, h, w_down_p, q_norm, kv_norm, lw, cos_t, sin_t, *, tm_want=1024):
    T, D = h.shape
    N = w_down_p.shape[1]
    tm = _row_tile(geo, tm_want)
    row = lambda n: pl.BlockSpec((tm, n), lambda i: (i, 0))
    return pl.pallas_call(
        _mla_down_kernel,
        grid=(T // tm,),
        in_specs=[row(D),
                  pl.BlockSpec((D, N), lambda i: (0, 0)),
                  _layer_spec(lw, (1, MLA_Q_RANK), lambda i: (0, 0)),
                  _layer_spec(lw, (1, MLA_KV_RANK), lambda i: (0, 0)),
                  _pos_spec(geo, tm), _pos_spec(geo, tm)],
        out_specs=[row(MLA_Q_RANK), row(MLA_KV_RANK), row(ROPE_PAD)],
        out_shape=[jax.ShapeDtypeStruct((T, MLA_Q_RANK), BF16),
                   jax.ShapeDtypeStruct((T, MLA_KV_RANK), BF16),
                   jax.ShapeDtypeStruct((T, ROPE_PAD), BF16)],
        compiler_params=_cparams(("parallel",)),
        name="mla_down",
    )(h, w_down_p, q_norm, kv_norm, cos_t, sin_t)


def _mla_q_kernel(cq_ref, w_ref, cos_ref, sin_ref, q_ref, *, qscale):
    a = cq_ref[...]
    cos = cos_ref[...] * qscale
    sin = sin_ref[...] * qscale
    for h in range(MLA_HEADS):
        lo = h * QK_PAD
        acc = jnp.dot(a, w_ref[:, lo:lo + QK_PAD], preferred_element_type=F32)
        q_ref[:, lo:lo + MLA_NOPE_DIM] = (acc[:, :MLA_NOPE_DIM] * qscale).astype(BF16)
        q_ref[:, lo + MLA_NOPE_DIM:lo + QK_PAD] = _rope(acc[:, MLA_NOPE_DIM:], cos, sin).astype(BF16)


def _mla_q(geo, cq, w_uq_p, cos_t, sin_t, *, tm_want=512):
    T, R = cq.shape
    N = w_uq_p.shape[1]
    tm = _row_tile(geo, tm_want)
    return pl.pallas_call(
        functools.partial(_mla_q_kernel, qscale=MLA_SCALE * LOG2E),
        grid=(T // tm,),
        in_specs=[pl.BlockSpec((tm, R), lambda i: (i, 0)),
                  pl.BlockSpec((R, N), lambda i: (0, 0)),
                  _pos_spec(geo, tm), _pos_spec(geo, tm)],
        out_specs=pl.BlockSpec((tm, N), lambda i: (i, 0)),
        out_shape=jax.ShapeDtypeStruct((T, N), BF16),
        compiler_params=_cparams(("parallel",)),
        name="mla_q_up",
    )(cq, w_uq_p, cos_t, sin_t)


def _mla_kv_kernel(ckv_ref, w_ref, kpe_ref, k_ref, v_ref):
    a = ckv_ref[...]
    kpe = kpe_ref[...]
    hw = MLA_NOPE_DIM + MLA_V_DIM
    for h in range(MLA_HEADS):
        acc = jnp.dot(a, w_ref[:, h * hw:(h + 1) * hw], preferred_element_type=F32)
        lo = h * QK_PAD
        k_ref[:, lo:lo + MLA_NOPE_DIM] = acc[:, :MLA_NOPE_DIM].astype(BF16)
        k_ref[:, lo + MLA_NOPE_DIM:lo + QK_PAD] = kpe
        v_ref[:, h * MLA_V_DIM:(h + 1) * MLA_V_DIM] = acc[:, MLA_NOPE_DIM:].astype(BF16)


def _mla_kv(geo, ckv, w_ukv, lw, kpe, *, tm_want=512):
    T, R = ckv.shape
    N = w_ukv.shape[2]
    tm = _row_tile(geo, tm_want)
    nk = MLA_HEADS * QK_PAD
    nv = MLA_HEADS * MLA_V_DIM
    return pl.pallas_call(
        _mla_kv_kernel,
        grid=(T // tm,),
        in_specs=[pl.BlockSpec((tm, R), lambda i: (i, 0)),
                  _layer_spec(lw, (R, N), lambda i: (0, 0)),
                  pl.BlockSpec((tm, ROPE_PAD), lambda i: (i, 0))],
        out_specs=[pl.BlockSpec((tm, nk), lambda i: (i, 0)),
                   pl.BlockSpec((tm, nv), lambda i: (i, 0))],
        out_shape=[jax.ShapeDtypeStruct((T, nk), BF16),
                   jax.ShapeDtypeStruct((T, nv), BF16)],
        compiler_params=_cparams(("parallel",)),
        name="mla_kv_up",
    )(ckv, w_ukv, kpe)


def _attn_kernel(q_ref, k_ref, v_ref, o_ref, *, tk):
    tq = q_ref.shape[0]
    for h in range(ATTN_HEADS):
        qc = slice(h * QK_PAD, (h + 1) * QK_PAD)
        vc = slice(h * MLA_V_DIM, (h + 1) * MLA_V_DIM)
        q = q_ref[:, qc]
        m = jnp.full((tq, 1), -jnp.inf, F32)
        l = jnp.zeros((tq, 1), F32)
        acc = jnp.zeros((tq, MLA_V_DIM), F32)
        for j in range(k_ref.shape[0] // tk):
            ks = slice(j * tk, (j + 1) * tk)
            s = lax.dot_general(q, k_ref[ks, qc], (((1,), (1,)), ((), ())), preferred_element_type=F32)
            m_new = jnp.maximum(m, jnp.max(s, axis=-1, keepdims=True))
            alpha = jnp.exp2(m - m_new)
            p = jnp.exp2(s - m_new)
            l = alpha * l + jnp.sum(p, axis=-1, keepdims=True)
            acc = alpha * acc + jnp.dot(p.astype(BF16), v_ref[ks, vc], preferred_element_type=F32)
            m = m_new
        o_ref[:, vc] = (acc / l).astype(o_ref.dtype)


def _attn(q, k, v, grp, *, tq_want=1024, tk_want=1024):
    B, S = grp.B, grp.S
    tq = min(tq_want, S)
    tk = min(tk_want, S)
    assert grp.row_off % S == 0 and MLA_HEADS % ATTN_HEADS == 0
    ob = grp.row_off // S
    oq = grp.row_off // tq
    nq = S // tq
    nh = ATTN_HEADS
    return pl.pallas_call(
        functools.partial(_attn_kernel, tk=tk),
        grid=(B, MLA_HEADS // nh, nq),
        in_specs=[pl.BlockSpec((tq, nh * QK_PAD), lambda b, h, i: (oq + b * nq + i, h)),
                  pl.BlockSpec((S, nh * QK_PAD), lambda b, h, i: (ob + b, h)),
                  pl.BlockSpec((S, nh * MLA_V_DIM), lambda b, h, i: (ob + b, h))],
        out_specs=pl.BlockSpec((tq, nh * MLA_V_DIM), lambda b, h, i: (b * nq + i, h)),
        out_shape=jax.ShapeDtypeStruct((B * S, MLA_HEADS * MLA_V_DIM), BF16),
        compiler_params=_cparams(("parallel", "parallel", "arbitrary")),
        name="mla_attn",
    )(q, k, v)


def _chan_dft_matrix(gd):
    k = jnp.arange(gd, dtype=jnp.int32)
    ang = ((k[:, None] * k[None, :]) % gd).astype(F32) * (2.0 * math.pi / gd)
    nrm = 1.0 / math.sqrt(gd)
    return jnp.concatenate([jnp.cos(ang) * nrm, -jnp.sin(ang) * nrm], axis=1).astype(BF16)


def _rope_tables(geo):
    half = MLA_ROPE_DIM // 2
    inv_freq = 1.0 / (ROPE_BASE ** (jnp.arange(0, MLA_ROPE_DIM, 2, dtype=F32) / MLA_ROPE_DIM))
    pos = jnp.arange(max(geo.S1, geo.S2), dtype=F32)
    ang = pos[:, None] * inv_freq[None, :]
    c, s = jnp.cos(ang), jnp.sin(ang)
    z = jnp.zeros_like(c)
    assert ROPE_PAD == 4 * half
    return jnp.concatenate([c, z, c, z], axis=1), jnp.concatenate([-s, z, s, z], axis=1)


def _spread_rope_cols(w):
    half = MLA_ROPE_DIM // 2
    z = jnp.zeros(w.shape[:-1] + (half,), w.dtype)
    return jnp.concatenate([w[..., :half], z, w[..., half:], z], axis=-1)


def kernel(x_prompt, x_sample, c_prompt, c_sample, ln_mix, ln_ffn, w_ada, b_ada, fnet_w_o, fnet_b_o, conv_w_in, conv_b_in, conv_w_dw, conv_b_dw, conv_ln_g, conv_ln_b, conv_w_out, conv_b_out, mla_w_down, mla_q_norm, mla_w_uq, mla_kv_norm, mla_w_ukv, mla_w_o, ffn_w_gate, ffn_w_up, ffn_w_down, final_norm):
    B1, S1, D = x_prompt.shape
    B2, S2, _ = x_sample.shape
    geo = Geo(B1, S1, B2, S2, D)
    g1, g2 = _groups(geo)
    depth = ln_mix.shape[0]
    NB = geo.NB

    x_parts = [x_prompt.reshape(geo.T1, D), x_sample.reshape(geo.T2, D)]

    c_all = jnp.concatenate([c_prompt, c_sample], axis=0)
    c_pad = jnp.pad(c_all, ((0, -NB % BF16_ROWS), (0, 0)))
    mod = _ada(c_pad, w_ada, b_ada)[:, :NB].reshape(depth, NB, 6, 1, D)

    bf = lambda w: w.astype(BF16)
    vec = lambda v: v.reshape(v.shape[0], 1, v.shape[1])
    ln_mix3, ln_ffn3 = vec(ln_mix), vec(ln_ffn)
    w_gate, w_up, w_dn = ffn_w_gate, ffn_w_up, bf(ffn_w_down)
    d_ff = ffn_w_gate.shape[2]
    full = dict(tn_want=D)
    h = None

    for i in range(depth):
        kind, j = i % N_MIXERS, i // N_MIXERS
        close = dict(emit=Emit(ln_ffn3, i, SC_F, SH_F, False), tm_want=512, **full)
        if kind == 0:
            cs = _chan_dft_matrix(D // FNET_GROUPS)
            f_parts = []
            for n, grp in enumerate((g1, g2)):
                if h is None:
                    vre, vim = _fnet_chan(x_parts[n], 0, grp, ln_mix3, i, mod, cs, normed=False)
                else:
                    vre, vim = _fnet_chan(h, grp.row_off, grp, None, i, None, cs, normed=True)
                f_parts.append(_fnet_seq(vre, vim, grp))
            x, h = _res(geo, f_parts, bf(fnet_w_o), j, vec(fnet_b_o), j, x_parts, i, mod, G_M, **close)
        elif kind == 1:
            u = _gated(h, conv_w_in, j, 0, conv_w_in, j, D, D, vec(conv_b_in), j, mode="glu")
            u = _conv(geo, u, conv_w_dw, j, vec(conv_b_dw), vec(conv_ln_g), vec(conv_ln_b))
            x, h = _res(geo, [u], bf(conv_w_out), j, vec(conv_b_out), j, x_parts, i, mod, G_M, **close)
        else:
            cos_t, sin_t = _rope_tables(geo)
            qk = MLA_Q_RANK + MLA_KV_RANK
            w_down_p = bf(jnp.concatenate([mla_w_down[j][:, :qk], _spread_rope_cols(mla_w_down[j][:, qk:])], axis=1))
            w_uq = mla_w_uq[j].reshape(MLA_Q_RANK, MLA_HEADS, MLA_QK_DIM)
            w_uq_p = bf(jnp.concatenate([w_uq[..., :MLA_NOPE_DIM], _spread_rope_cols(w_uq[..., MLA_NOPE_DIM:])],
                                        axis=-1).reshape(MLA_Q_RANK, MLA_HEADS * QK_PAD))
            cq, ckv, kpe = _mla_down(geo, h, w_down_p, vec(mla_q_norm), vec(mla_kv_norm), j, cos_t, sin_t)
            q = _mla_q(geo, cq, w_uq_p, cos_t, sin_t)
            k, v = _mla_kv(geo, ckv, bf(mla_w_ukv), j, kpe)
            o_parts = [_attn(q, k, v, g1), _attn(q, k, v, g2)]
            x, h = _res(geo, o_parts, bf(mla_w_o), j, None, 0, x_parts, i, mod, G_M, **close)

        u = _gated(h, w_gate, i, 0, w_up, i, 0, d_ff, None, 0, mode="swiglu")
        ffn = (geo, [u], w_dn, i, None, 0, [x], i, mod, G_F)
        if i + 1 < depth:
            x, h = _res(*ffn, emit=Emit(ln_mix3, i + 1, SC_M, SH_M, False), tm_want=256, **full)
            x_parts = [x]

    last = Emit(final_norm.reshape(1, 1, D), 0, 0, 0, True)
    y1, y2 = (_res(*ffn, emit=last, rows=(g.row_off, g.B * g.S), tm_want=256, **full) for g in (g1, g2))
    return (y1.reshape(B1, S1, D), y2.reshape(B2, S2, D))
```
